```python
import math
import jax, jax.numpy as jnp
from jax import lax
import numpy as np

D_MODEL = 1024
BATCH = 16
SEQ = 2048
DEPTH = 1
DEC_BATCH = 128
DEC_SEQ = 8
PAST_LEN = 8192
PAGE_SIZE = 128

ATT_HEADS = 8
N_KV_HEADS = 2
HEAD_DIM = 64
ATT_WIDTH = ATT_HEADS * HEAD_DIM
ROPE_THETA = 500000.0
IDX_HEADS = 8
IDX_DIM = 64
TOPK_MAX = 256
Q_BLOCK = 128
GDN_HEADS = 4
GDN_DK = 128
GDN_DV = 128
GDN_WIDTH = GDN_HEADS * GDN_DV
GDN_CONV_DIM = 2 * GDN_HEADS * GDN_DK + GDN_WIDTH
CONV_K = 4
GDN_CHUNK = 64
MIX_WIDTH = ATT_WIDTH + GDN_WIDTH
D_FF = 2816
LN_EPS = 1e-5
NORM_EPS = 1e-6
DEEP_ALPHA = (2 * DEPTH) ** 0.25
DEEP_BETA = (8 * DEPTH) ** -0.25
SPLITS = (ATT_WIDTH, N_KV_HEADS * HEAD_DIM, N_KV_HEADS * HEAD_DIM, IDX_HEADS * IDX_DIM, IDX_DIM, IDX_HEADS,
          GDN_CONV_DIM, GDN_WIDTH, GDN_HEADS, GDN_HEADS)
IN_WIDTH = sum(SPLITS)
SPLIT_POINTS = tuple(int(s) for s in np.cumsum(SPLITS)[:-1])

kernel_name = 'dsa_gdn_macaron_deepnorm_step'


def layer_norm(x, g, b):
    xf = x.astype(jnp.float32)
    mu = jnp.mean(xf, -1, keepdims=True)
    var = jnp.mean(jnp.square(xf - mu), -1, keepdims=True)
    return ((xf - mu) * lax.rsqrt(var + LN_EPS) * g.astype(jnp.float32) + b.astype(jnp.float32)).astype(x.dtype)


def swiglu(x, w_gate, w_up, w_down):
    return (jax.nn.silu(x @ w_gate) * (x @ w_up)) @ w_down


def ffn_half_step(x, w_gate, w_up, w_down, g, b):
    return layer_norm(DEEP_ALPHA * x + 0.5 * swiglu(x, w_gate, w_up, w_down), g, b)


def mixer_residual(x, att, gdn, w_o, g, b):
    return layer_norm(DEEP_ALPHA * x + jnp.concatenate([att, gdn], -1) @ w_o, g, b)


def rope_partial(x, pos):
    d_rot = x.shape[-1] // 4
    half = d_rot // 2
    inv_freq = ROPE_THETA ** (-jnp.arange(half, dtype=jnp.float32) / half)
    ang = pos.astype(jnp.float32)[:, None] * inv_freq[None, :]
    cos = jnp.cos(ang)[None, :, None, :]
    sin = jnp.sin(ang)[None, :, None, :]
    xr = x[..., :d_rot].astype(jnp.float32)
    x1, x2 = xr[..., :half], xr[..., half:]
    rot = jnp.concatenate([x1 * cos - x2 * sin, x2 * cos + x1 * sin], -1).astype(x.dtype)
    return jnp.concatenate([rot, x[..., d_rot:]], -1)


def mix_projection(h, w_in, pos):
    B, T, _ = h.shape
    q, k, v, qi, ki, wi, qkv, z, b_g, a_g = jnp.split(h @ w_in, SPLIT_POINTS, axis=-1)
    q = rope_partial(q.reshape(B, T, ATT_HEADS, HEAD_DIM), pos)
    k = rope_partial(k.reshape(B, T, N_KV_HEADS, HEAD_DIM), pos)
    v = v.reshape(B, T, N_KV_HEADS, HEAD_DIM)
    qi = rope_partial(qi.reshape(B, T, IDX_HEADS, IDX_DIM), pos)
    ki = rope_partial(ki.reshape(B, T, 1, IDX_DIM), pos)[:, :, 0]
    wi = wi * IDX_HEADS ** -0.5
    return q, k, v, qi, ki, wi, qkv, z, b_g, a_g


def indexer_scores(qi, wi, ki):
    s = jnp.einsum('bqhd,bld->bqhl', qi.astype(jnp.float32), ki.astype(jnp.float32)) * IDX_DIM ** -0.5
    return jnp.einsum('bqhl,bqh->bql', jax.nn.relu(s), wi.astype(jnp.float32))


def sparse_attend(q, ks, vs, valid):
    B, Q, H, d = q.shape
    G = H // N_KV_HEADS
    qg = q.reshape(B, Q, N_KV_HEADS, G, d).astype(jnp.float32)
    s = jnp.einsum('bqkgd,bqnkd->bqkgn', qg, ks.astype(jnp.float32)) * d ** -0.5
    p = jax.nn.softmax(jnp.where(valid[:, :, None, None, :], s, -jnp.inf), axis=-1)
    o = jnp.einsum('bqkgn,bqnkd->bqkgd', p, vs.astype(jnp.float32))
    return o.reshape(B, Q, H * d).astype(q.dtype)


def dsa_prompt(q, k, v, qi, ki, wi):
    B, S = q.shape[:2]
    nblk = S // Q_BLOCK
    ksel = min(TOPK_MAX, S // 4)
    key_pos = jnp.arange(S)
    bidx = jnp.arange(B)[:, None, None]

    def blocks(a):
        return jnp.swapaxes(a.reshape((B, nblk, Q_BLOCK) + a.shape[2:]), 0, 1)

    def attend_block(args):
        q_b, qi_b, wi_b, pos_b = args
        score = indexer_scores(qi_b, wi_b, ki)
        score = jnp.where(key_pos[None, None, :] <= pos_b[None, :, None], score, -jnp.inf)
        _, idx = lax.top_k(score, ksel)
        return sparse_attend(q_b, k[bidx, idx], v[bidx, idx], idx <= pos_b[None, :, None])

    out = lax.map(attend_block, (blocks(q), blocks(qi), blocks(wi), key_pos.reshape(nblk, Q_BLOCK)))
    return jnp.swapaxes(out, 0, 1).reshape(B, S, ATT_WIDTH)


def dsa_sample(q, k, v, qi, ki, wi, cache_k, cache_v, cache_kidx, page_table, layer):
    B, T = q.shape[:2]
    page = cache_k.shape[2]
    past = page_table.shape[1] * page
    L = past + T
    ksel = min(TOPK_MAX, L // 4)
    ki_past = cache_kidx[layer, page_table].reshape(B, past, IDX_DIM)
    ki_all = jnp.concatenate([ki_past, ki.astype(ki_past.dtype)], 1)
    pos_q = past + jnp.arange(T)
    score = indexer_scores(qi, wi, ki_all)
    score = jnp.where(jnp.arange(L)[None, None, :] <= pos_q[None, :, None], score, -jnp.inf)
    _, idx = lax.top_k(score, ksel)
    bidx = jnp.arange(B)[:, None, None]
    from_past = (idx < past)[..., None, None]
    pidx = jnp.minimum(idx, past - 1)
    phys = page_table[bidx, pidx // page]
    off = pidx % page
    nidx = jnp.clip(idx - past, 0, T - 1)
    ks = jnp.where(from_past, cache_k[layer, phys, off], k[bidx, nidx].astype(cache_k.dtype))
    vs = jnp.where(from_past, cache_v[layer, phys, off], v[bidx, nidx].astype(cache_v.dtype))
    return sparse_attend(q, ks, vs, idx <= pos_q[None, :, None])


def l2norm(x):
    xf = x.astype(jnp.float32)
    return xf * lax.rsqrt(jnp.sum(jnp.square(xf), -1, keepdims=True) + NORM_EPS)


def gated_delta_chunked(q, k, v, g, beta, s0):
    B, T, H, dk = q.shape
    dv = v.shape[-1]
    C = GDN_CHUNK
    n = -(-T // C)
    pad = n * C - T

    def to_chunks(a):
        a = jnp.pad(a.astype(jnp.float32), [(0, 0), (0, pad)] + [(0, 0)] * (a.ndim - 2))
        return jnp.swapaxes(jnp.moveaxis(a.reshape((B, n, C) + a.shape[2:]), 1, 0), 2, 3)

    qc = to_chunks(q) * dk ** -0.5
    kc, vc, gc, bc = to_chunks(k), to_chunks(v), to_chunks(g), to_chunks(beta)
    gc = jnp.cumsum(gc, -1)
    causal = jnp.tril(jnp.ones((C, C), bool))
    strict = jnp.tril(jnp.ones((C, C), bool), -1)
    decay = jnp.exp(jnp.where(causal, gc[..., :, None] - gc[..., None, :], -jnp.inf))
    kb = kc * bc[..., None]
    m = jnp.where(strict, jnp.einsum('nbhid,nbhjd->nbhij', kb, kc) * decay, 0.0)
    rhs = jnp.concatenate([vc * bc[..., None], kb * jnp.exp(gc)[..., None]], -1)
    sol = lax.linalg.triangular_solve(m + jnp.eye(C, dtype=jnp.float32), rhs,
                                      left_side=True, lower=True, unit_diagonal=True)
    u, w = sol[..., :dv], sol[..., dv:]
    qk = jnp.where(causal, jnp.einsum('nbhid,nbhjd->nbhij', qc, kc) * decay, 0.0)

    def step(S, xs):
        q_i, k_i, u_i, w_i, g_i, qk_i = xs
        v_new = u_i - jnp.einsum('bhck,bhkv->bhcv', w_i, S)
        o_i = (jnp.einsum('bhck,bhkv->bhcv', q_i * jnp.exp(g_i)[..., None], S)
               + jnp.einsum('bhij,bhjv->bhiv', qk_i, v_new))
        g_last = g_i[..., -1]
        S = (S * jnp.exp(g_last)[..., None, None]
             + jnp.einsum('bhck,bhcv->bhkv', k_i * jnp.exp(g_last[..., None] - g_i)[..., None], v_new))
        return S, o_i

    S, o = lax.scan(step, s0.astype(jnp.float32), (qc, kc, u, w, gc, qk))
    o = jnp.moveaxis(jnp.swapaxes(o, 2, 3), 0, 1).reshape(B, n * C, H, dv)[:, :T]
    return o, S


def gdn_branch(qkv, z, b_g, a_g, conv_buf, s0, conv_w, a_log, dt_bias, norm_w):
    B, T, _ = qkv.shape
    xin = jnp.concatenate([conv_buf.astype(qkv.dtype), qkv], 1)
    y = lax.conv_general_dilated(xin, conv_w[:, None, :].astype(qkv.dtype), window_strides=(1,), padding='VALID',
                                 dimension_numbers=('NWC', 'WIO', 'NWC'), feature_group_count=GDN_CONV_DIM)
    y = jax.nn.silu(y)
    nq = GDN_HEADS * GDN_DK
    q = l2norm(y[..., :nq].reshape(B, T, GDN_HEADS, GDN_DK))
    k = l2norm(y[..., nq:2 * nq].reshape(B, T, GDN_HEADS, GDN_DK))
    v = y[..., 2 * nq:].reshape(B, T, GDN_HEADS, GDN_DV)
    beta = jax.nn.sigmoid(b_g.astype(jnp.float32))
    g = -jnp.exp(a_log.astype(jnp.float32)) * jax.nn.softplus(a_g.astype(jnp.float32) + dt_bias.astype(jnp.float32))
    o, s = gated_delta_chunked(q, k, v, g, beta, s0)
    o = o * lax.rsqrt(jnp.mean(jnp.square(o), -1, keepdims=True) + NORM_EPS) * norm_w.astype(jnp.float32)
    o = o * jax.nn.silu(z.reshape(B, T, GDN_HEADS, GDN_DV).astype(jnp.float32))
    return o.reshape(B, T, GDN_WIDTH).astype(qkv.dtype), s.astype(s0.dtype), xin[:, T:]


def setup_inputs(seed: int = 0) -> dict:
    key = jax.random.key(seed)
    ks = jax.random.split(key, 32)
    f32 = jnp.float32
    n_pages = PAST_LEN // PAGE_SIZE
    n_pool = (DEC_BATCH * n_pages * 5) // 4

    def nrm(k, shape, scale):
        return jax.random.normal(k, shape, f32) * scale

    def gain(k, n):
        return 1.0 + 0.02 * jax.random.normal(k, (DEPTH, n), f32)

    def bias(k, n):
        return 0.02 * jax.random.normal(k, (DEPTH, n), f32)

    dt = jnp.exp(jax.random.uniform(ks[14], (DEPTH, GDN_HEADS), f32, math.log(1e-3), math.log(1e-1)))
    page_table = jax.random.permutation(ks[7], n_pool)[:DEC_BATCH * n_pages].reshape(DEC_BATCH, n_pages).astype(jnp.int32)
    return {
        'x_prompt': nrm(ks[0], (BATCH, SEQ, D_MODEL), 1.0),
        'x_sample': nrm(ks[1], (DEC_BATCH, DEC_SEQ, D_MODEL), 1.0),
        'cache_k': nrm(ks[2], (DEPTH, n_pool, PAGE_SIZE, N_KV_HEADS, HEAD_DIM), 1.0),
        'cache_v': nrm(ks[3], (DEPTH, n_pool, PAGE_SIZE, N_KV_HEADS, HEAD_DIM), 1.0),
        'cache_kidx': nrm(ks[4], (DEPTH, n_pool, PAGE_SIZE, IDX_DIM), 1.0),
        'state_gdn': nrm(ks[5], (DEPTH, DEC_BATCH, GDN_HEADS, GDN_DK, GDN_DV), 0.1),
        'state_conv': nrm(ks[6], (DEPTH, DEC_BATCH, CONV_K - 1, GDN_CONV_DIM), 1.0),
        'page_table': page_table,
        'ffn1_w_gate': nrm(ks[8], (DEPTH, D_MODEL, D_FF), D_MODEL ** -0.5),
        'ffn1_w_up': nrm(ks[9], (DEPTH, D_MODEL, D_FF), D_MODEL ** -0.5),
        'ffn1_w_down': nrm(ks[10], (DEPTH, D_FF, D_MODEL), DEEP_BETA * D_FF ** -0.5),
        'ln1_g': gain(ks[11], D_MODEL),
        'ln1_b': bias(ks[12], D_MODEL),
        'w_in': nrm(ks[13], (DEPTH, D_MODEL, IN_WIDTH), D_MODEL ** -0.5),
        'conv_w': nrm(ks[15], (DEPTH, CONV_K, GDN_CONV_DIM), CONV_K ** -0.5),
        'a_log': jnp.log(jax.random.uniform(ks[16], (DEPTH, GDN_HEADS), f32, 1.0, 16.0)),
        'dt_bias': dt + jnp.log(-jnp.expm1(-dt)),
        'gdn_norm_w': gain(ks[17], GDN_DV),
        'w_o': nrm(ks[18], (DEPTH, MIX_WIDTH, D_MODEL), DEEP_BETA * MIX_WIDTH ** -0.5),
        'ln2_g': gain(ks[19], D_MODEL),
        'ln2_b': bias(ks[20], D_MODEL),
        'ffn2_w_gate': nrm(ks[21], (DEPTH, D_MODEL, D_FF), D_MODEL ** -0.5),
        'ffn2_w_up': nrm(ks[22], (DEPTH, D_MODEL, D_FF), D_MODEL ** -0.5),
        'ffn2_w_down': nrm(ks[23], (DEPTH, D_FF, D_MODEL), DEEP_BETA * D_FF ** -0.5),
        'ln3_g': gain(ks[24], D_MODEL),
        'ln3_b': bias(ks[25], D_MODEL),
    }


def reference(x_prompt, x_sample, cache_k, cache_v, cache_kidx, state_gdn, state_conv, page_table,
              ffn1_w_gate, ffn1_w_up, ffn1_w_down, ln1_g, ln1_b, w_in, conv_w, a_log, dt_bias, gdn_norm_w,
              w_o, ln2_g, ln2_b, ffn2_w_gate, ffn2_w_up, ffn2_w_down, ln3_g, ln3_b):
    B, S, _ = x_prompt.shape
    T = x_sample.shape[1]
    past = page_table.shape[1] * cache_k.shape[2]
    pos_p = jnp.arange(S)
    pos_s = past + jnp.arange(T)
    xp, xs = x_prompt, x_sample
    kp_l, vp_l, ip_l, sp_l, cp_l = [], [], [], [], []
    ks_l, vs_l, is_l, ss_l, cs_l = [], [], [], [], []
    for l in range(DEPTH):
        xp = ffn_half_step(xp, ffn1_w_gate[l], ffn1_w_up[l], ffn1_w_down[l], ln1_g[l], ln1_b[l])
        xs = ffn_half_step(xs, ffn1_w_gate[l], ffn1_w_up[l], ffn1_w_down[l], ln1_g[l], ln1_b[l])
        q, k, v, qi, ki, wi, qkv, z, b_g, a_g = mix_projection(xp, w_in[l], pos_p)
        att = dsa_prompt(q, k, v, qi, ki, wi)
        gdn, s_new, c_new = gdn_branch(qkv, z, b_g, a_g, jnp.zeros((B, CONV_K - 1, GDN_CONV_DIM), xp.dtype),
                                       jnp.zeros((B, GDN_HEADS, GDN_DK, GDN_DV), jnp.float32),
                                       conv_w[l], a_log[l], dt_bias[l], gdn_norm_w[l])
        xp = mixer_residual(xp, att, gdn, w_o[l], ln2_g[l], ln2_b[l])
        kp_l.append(k); vp_l.append(v); ip_l.append(ki); sp_l.append(s_new); cp_l.append(c_new)
        q, k, v, qi, ki, wi, qkv, z, b_g, a_g = mix_projection(xs, w_in[l], pos_s)
        att = dsa_sample(q, k, v, qi, ki, wi, cache_k, cache_v, cache_kidx, page_table, l)
        gdn, s_new, c_new = gdn_branch(qkv, z, b_g, a_g, state_conv[l], state_gdn[l],
                                       conv_w[l], a_log[l], dt_bias[l], gdn_norm_w[l])
        xs = mixer_residual(xs, att, gdn, w_o[l], ln2_g[l], ln2_b[l])
        ks_l.append(k); vs_l.append(v); is_l.append(ki); ss_l.append(s_new); cs_l.append(c_new)
        xp = ffn_half_step(xp, ffn2_w_gate[l], ffn2_w_up[l], ffn2_w_down[l], ln3_g[l], ln3_b[l])
        xs = ffn_half_step(xs, ffn2_w_gate[l], ffn2_w_up[l], ffn2_w_down[l], ln3_g[l], ln3_b[l])
    return (xp, xs,
            jnp.stack(kp_l), jnp.stack(vp_l), jnp.stack(ip_l), jnp.stack(sp_l), jnp.stack(cp_l),
            jnp.stack(ks_l), jnp.stack(vs_l), jnp.stack(is_l), jnp.stack(ss_l), jnp.stack(cs_l))
```

```python
import functools
import math

import jax
import jax.numpy as jnp
from jax import lax
from jax.experimental import pallas as pl
from jax.experimental.pallas import tpu as pltpu

F32 = jnp.float32
BF16 = jnp.bfloat16
HI = lax.Precision.HIGHEST

LANES = 128
VMEM_LIMIT = 56 * 1024 * 1024

D_MODEL = 1024
DEPTH = 1
ATT_HEADS = 8
N_KV_HEADS = 2
HEAD_DIM = 64
ATT_WIDTH = ATT_HEADS * HEAD_DIM
ROPE_THETA = 500000.0
IDX_HEADS = 8
IDX_DIM = 64
TOPK_MAX = 256
GDN_HEADS = 4
GDN_DK = 128
GDN_DV = 128
GDN_WIDTH = GDN_HEADS * GDN_DV
GDN_CONV_DIM = 2 * GDN_HEADS * GDN_DK + GDN_WIDTH
CONV_K = 4
GDN_CHUNK = 64
LN_EPS = 1e-5
NORM_EPS = 1e-6
DEEP_ALPHA = (2 * DEPTH) ** 0.25
SPLITS = (ATT_WIDTH, N_KV_HEADS * HEAD_DIM, N_KV_HEADS * HEAD_DIM, IDX_HEADS * IDX_DIM, IDX_DIM, IDX_HEADS,
          GDN_CONV_DIM, GDN_WIDTH, GDN_HEADS, GDN_HEADS)
ROPE_W = 1280
N_BISECT = 20

NEG_INF = float("-inf")
POS_INF = float("inf")


def _cparams(sem):
    return pltpu.CompilerParams(dimension_semantics=sem, vmem_limit_bytes=VMEM_LIMIT)


def _layer_norm_rows(y, g, b):
    mu = jnp.mean(y, axis=-1, keepdims=True)
    d = y - mu
    var = jnp.mean(d * d, axis=-1, keepdims=True)
    return d * lax.rsqrt(var + LN_EPS) * g + b


def _silu(x):
    return x * jax.nn.sigmoid(x)


def _ffn_body(x_ref, wg_ref, wu_ref, wd_ref, g_ref, b_ref, o_ref, acc_ref):
    j = pl.program_id(1)
    xb = x_ref[...].astype(BF16)
    hg = jnp.dot(xb, wg_ref[...], preferred_element_type=F32)
    hu = jnp.dot(xb, wu_ref[...], preferred_element_type=F32)
    h = _silu(hg) * hu
    part = jnp.dot(h.astype(BF16), wd_ref[...], preferred_element_type=F32)

    @pl.when(j == 0)
    def _():
        acc_ref[...] = part

    @pl.when(j > 0)
    def _():
        acc_ref[...] += part

    @pl.when(j == pl.num_programs(1) - 1)
    def _():
        y = DEEP_ALPHA * x_ref[...] + 0.5 * acc_ref[...]
        o_ref[...] = _layer_norm_rows(y, g_ref[...], b_ref[...])


def _ffn_half_step(x, wg, wu, wd, g, b):
    M, D = x.shape
    FF = wg.shape[1]
    tm = min(512, M)
    tf = FF // 2 if (FF // 2) % LANES == 0 else FF
    return pl.pallas_call(
        _ffn_body,
        grid=(M // tm, FF // tf),
        in_specs=[
            pl.BlockSpec((tm, D), lambda i, j: (i, 0)),
            pl.BlockSpec((D, tf), lambda i, j: (0, j)),
            pl.BlockSpec((D, tf), lambda i, j: (0, j)),
            pl.BlockSpec((tf, D), lambda i, j: (j, 0)),
            pl.BlockSpec((1, D), lambda i, j: (0, 0)),
            pl.BlockSpec((1, D), lambda i, j: (0, 0)),
        ],
        out_specs=pl.BlockSpec((tm, D), lambda i, j: (i, 0)),
        out_shape=jax.ShapeDtypeStruct((M, D), F32),
        scratch_shapes=[pltpu.VMEM((tm, D), F32)],
        compiler_params=_cparams(("parallel", "arbitrary")),
        name="ffn_half_step",
    )(x, wg, wu, wd, g, b)


def _proj_body(x_ref, wr_ref, wv_ref, ws_ref, wqkv_ref, wz_ref, cos_ref, sa_ref, sb_ref,
               q_ref, qi_ref, k_ref, ki_ref, v_ref, kt_ref, kit_ref, small_ref, qkv_ref, z_ref):
    xb = x_ref[...].astype(BF16)
    r = jnp.dot(xb, wr_ref[...], preferred_element_type=F32)
    c = cos_ref[...]
    sa = sa_ref[...]
    sb = sb_ref[...]

    def rope(slab):
        return slab * c + pltpu.roll(slab, LANES - 8, 1) * sa + pltpu.roll(slab, 8, 1) * sb

    for s in range(ATT_WIDTH // LANES):
        q_ref[:, s * LANES:(s + 1) * LANES] = rope(r[:, s * LANES:(s + 1) * LANES])
    off = ATT_WIDTH
    for s in range(IDX_HEADS * IDX_DIM // LANES):
        qi_ref[:, s * LANES:(s + 1) * LANES] = rope(r[:, off + s * LANES:off + (s + 1) * LANES])
    off += IDX_HEADS * IDX_DIM
    kr = rope(r[:, off:off + LANES])
    k_ref[...] = kr
    kt_ref[...] = kr.T
    kir = rope(r[:, off + LANES:off + 2 * LANES])
    ki_ref[...] = kir[:, :IDX_DIM]
    kit_ref[...] = kir.T[:IDX_DIM, :]
    v_ref[...] = jnp.dot(xb, wv_ref[...], preferred_element_type=F32)
    sm = jnp.dot(xb, ws_ref[...], preferred_element_type=F32)
    lane = lax.broadcasted_iota(jnp.int32, sm.shape, 1)
    small_ref[...] = jnp.where(lane < IDX_HEADS, sm * (IDX_HEADS ** -0.5), sm)
    qkv_ref[...] = jnp.dot(xb, wqkv_ref[...], preferred_element_type=F32)
    z_ref[...] = jnp.dot(xb, wz_ref[...], preferred_element_type=F32)


def _mix_projection(x, w, tabs, n_tab_blocks):
    M, D = x.shape
    tm = min(512, M)
    wr, wv, ws, wqkv, wz = w
    cos_t, sa_t, sb_t = tabs

    def full(a):
        return pl.BlockSpec(a.shape, lambda i: (0, 0))

    def rows(width):
        return pl.BlockSpec((tm, width), lambda i: (i, 0))

    tab_spec = pl.BlockSpec((tm, LANES), lambda i: (i % n_tab_blocks, 0))
    out_shapes = [
        jax.ShapeDtypeStruct((M, ATT_WIDTH), F32),
        jax.ShapeDtypeStruct((M, IDX_HEADS * IDX_DIM), F32),
        jax.ShapeDtypeStruct((M, N_KV_HEADS * HEAD_DIM), F32),
        jax.ShapeDtypeStruct((M, IDX_DIM), F32),
        jax.ShapeDtypeStruct((M, N_KV_HEADS * HEAD_DIM), F32),
        jax.ShapeDtypeStruct((N_KV_HEADS * HEAD_DIM, M), F32),
        jax.ShapeDtypeStruct((IDX_DIM, M), F32),
        jax.ShapeDtypeStruct((M, LANES), F32),
        jax.ShapeDtypeStruct((M, GDN_CONV_DIM), F32),
        jax.ShapeDtypeStruct((M, GDN_WIDTH), F32),
    ]
    out_specs = [
        rows(ATT_WIDTH), rows(IDX_HEADS * IDX_DIM), rows(N_KV_HEADS * HEAD_DIM), rows(IDX_DIM),
        rows(N_KV_HEADS * HEAD_DIM),
        pl.BlockSpec((N_KV_HEADS * HEAD_DIM, tm), lambda i: (0, i)),
        pl.BlockSpec((IDX_DIM, tm), lambda i: (0, i)),
        rows(LANES), rows(GDN_CONV_DIM), rows(GDN_WIDTH),
    ]
    return pl.pallas_call(
        _proj_body,
        grid=(M // tm,),
        in_specs=[rows(D), full(wr), full(wv), full(ws), full(wqkv), full(wz), tab_spec, tab_spec, tab_spec],
        out_specs=out_specs,
        out_shape=out_shapes,
        compiler_params=_cparams(("parallel",)),
        name="mix_projection",
    )(x, wr, wv, ws, wqkv, wz, cos_t, sa_t, sb_t)


def _count_ge(xm, thr):
    return jnp.sum(jnp.where(xm >= thr, 1.0, 0.0), axis=-1, keepdims=True)


def _prefix_count(eq_f32):
    R, L = eq_f32.shape
    ri = lax.broadcasted_iota(jnp.int32, (LANES, LANES), 0)
    ci = lax.broadcasted_iota(jnp.int32, (LANES, LANES), 1)
    tri = jnp.where(ri <= ci, 1.0, 0.0).astype(BF16)
    offset = jnp.zeros((R, 1), F32)
    pieces = []
    for blk in range(L // LANES):
        e = eq_f32[:, blk * LANES:(blk + 1) * LANES].astype(BF16)
        loc = jnp.dot(e, tri, preferred_element_type=F32)
        pieces.append(loc + offset)
        offset = offset + loc[:, LANES - 1:LANES]
    return jnp.concatenate(pieces, axis=1)


def _topk_bias(xm, k):
    kf = float(k)
    rowmax = jnp.max(xm, axis=-1, keepdims=True)
    rowmin = jnp.min(jnp.where(xm == NEG_INF, POS_INF, xm), axis=-1, keepdims=True)
    lo0 = rowmin
    hi0 = rowmax + jnp.abs(rowmax) * (2.0 ** -10) + 1.0

    def bisect(_, c):
        lo, hi = c
        mid = 0.5 * (lo + hi)
        ge = _count_ge(xm, mid) >= kf
        return jnp.where(ge, mid, lo), jnp.where(ge, hi, mid)

    lo, hi = lax.fori_loop(0, N_BISECT, bisect, (lo0, hi0))
    c_lo = _count_ge(xm, lo)
    pending = jnp.where(c_lo > kf, 1.0, 0.0)

    def cond(c):
        return jnp.max(c[3]) > 0.0

    def body(c):
        lo, hi, c_lo, pending = c
        t = jnp.max(jnp.where(xm < hi, xm, NEG_INF), axis=-1, keepdims=True)
        ct = _count_ge(xm, t)
        hit = ct >= kf
        live = pending > 0.0
        upd = jnp.logical_and(live, hit)
        lo = jnp.where(upd, t, lo)
        c_lo = jnp.where(upd, ct, c_lo)
        hi = jnp.where(jnp.logical_and(live, jnp.logical_not(hit)), t, hi)
        pending = jnp.where(hit, 0.0, pending)
        return lo, hi, c_lo, pending

    lo, hi, c_lo, pending = lax.while_loop(cond, body, (lo, hi, c_lo, pending))

    def plain():
        return jnp.where(xm >= lo, 0.0, NEG_INF)

    def with_ties():
        gt = xm > lo
        eq = xm == lo
        n_gt = jnp.sum(jnp.where(gt, 1.0, 0.0), axis=-1, keepdims=True)
        rank = _prefix_count(jnp.where(eq, 1.0, 0.0))
        keep = jnp.logical_or(gt, jnp.logical_and(eq, rank <= kf - n_gt))
        return jnp.where(keep, 0.0, NEG_INF)

    return lax.cond(jnp.max(c_lo) > kf, with_ties, plain)


def _indexer_scores(qi_bf, ki_t_bf, wi):
    ws = wi * (IDX_DIM ** -0.5)
    score = None
    for h in range(IDX_HEADS):
        s = jnp.dot(qi_bf[:, h * IDX_DIM:(h + 1) * IDX_DIM], ki_t_bf, preferred_element_type=F32)
        term = jnp.maximum(s, 0.0) * ws[:, h:h + 1]
        score = term if score is None else score + term
    return score


def _dsa_prompt_body(q_ref, qi_ref, sm_ref, kt_ref, kit_ref, v_ref, o_ref, *, tq, ksel):
    i = pl.program_id(1)
    S = kt_ref.shape[1]
    q_pos = i * tq + lax.broadcasted_iota(jnp.int32, (tq, S), 0)
    k_pos = lax.broadcasted_iota(jnp.int32, (tq, S), 1)
    valid = k_pos <= q_pos
    score = _indexer_scores(qi_ref[...].astype(BF16), kit_ref[...].astype(BF16), sm_ref[...])
    bias = _topk_bias(jnp.where(valid, score, NEG_INF), ksel)

    kt = kt_ref[...].astype(BF16)
    vb = v_ref[...].astype(BF16)
    qb = (q_ref[...] * (HEAD_DIM ** -0.5)).astype(BF16)
    group = ATT_HEADS // N_KV_HEADS
    lane = lax.broadcasted_iota(jnp.int32, (tq, LANES), 1)
    outs = []
    for h in range(ATT_HEADS):
        kv = h // group
        s = jnp.dot(qb[:, h * HEAD_DIM:(h + 1) * HEAD_DIM], kt[kv * HEAD_DIM:(kv + 1) * HEAD_DIM, :],
                    preferred_element_type=F32) + bias
        m = jnp.max(s, axis=-1, keepdims=True)
        p = jnp.exp(s - m)
        l = jnp.sum(p, axis=-1, keepdims=True)
        o = jnp.dot(p.astype(BF16), vb, preferred_element_type=F32) / l
        outs.append(o)
    for pair in range(ATT_HEADS // 2):
        h0, h1 = 2 * pair, 2 * pair + 1
        kv = h0 // group
        a, b = outs[h0], outs[h1]
        if kv == 0:
            b = pltpu.roll(b, HEAD_DIM, 1)
        else:
            a = pltpu.roll(a, HEAD_DIM, 1)
        o_ref[:, pair * LANES:(pair + 1) * LANES] = jnp.where(lane < HEAD_DIM, a, b)


def _dsa_prompt(q, qi, small, kt, kit, v, B, S):
    M = B * S
    tq = 128
    nq = S // tq
    ksel = min(TOPK_MAX, S // 4)
    return pl.pallas_call(
        functools.partial(_dsa_prompt_body, tq=tq, ksel=ksel),
        grid=(B, nq),
        in_specs=[
            pl.BlockSpec((tq, ATT_WIDTH), lambda b, i: (b * nq + i, 0)),
            pl.BlockSpec((tq, IDX_HEADS * IDX_DIM), lambda b, i: (b * nq + i, 0)),
            pl.BlockSpec((tq, LANES), lambda b, i: (b * nq + i, 0)),
            pl.BlockSpec((N_KV_HEADS * HEAD_DIM, S), lambda b, i: (0, b)),
            pl.BlockSpec((IDX_DIM, S), lambda b, i: (0, b)),
            pl.BlockSpec((S, N_KV_HEADS * HEAD_DIM), lambda b, i: (b, 0)),
        ],
        out_specs=pl.BlockSpec((tq, ATT_WIDTH), lambda b, i: (b * nq + i, 0)),
        out_shape=jax.ShapeDtypeStruct((M, ATT_WIDTH), F32),
        compiler_params=_cparams(("parallel", "arbitrary")),
        name="dsa_prompt",
    )(q, qi, small, kt, kit, v)


def _dsa_sample_body(pt_ref, qi_ref, wi_ref, q_ref, kin_ref, kn_ref, vn_ref, cki_hbm, ck_hbm, cv_hbm, o_ref,
                     kibuf, kbuf, vbuf, sem, *, n_pages, page, t_new, ksel):
    b = pl.program_id(0)
    nb = pl.num_programs(0)
    slot = lax.rem(b, 2)
    past = n_pages * page
    lp = kibuf.shape[1]

    def page_copies(bb, sl, p):
        pg = pt_ref[bb, p]
        rows = pl.ds(pl.multiple_of(p * page, page), page)
        return (pltpu.make_async_copy(cki_hbm.at[pg], kibuf.at[sl, rows], sem.at[sl, 0]),
                pltpu.make_async_copy(ck_hbm.at[pg], kbuf.at[sl, rows], sem.at[sl, 1]),
                pltpu.make_async_copy(cv_hbm.at[pg], vbuf.at[sl, rows], sem.at[sl, 2]))

    def start_all(bb, sl):
        def f(p, carry):
            for cp in page_copies(bb, sl, p):
                cp.start()
            return carry
        lax.fori_loop(0, n_pages, f, 0)

    def wait_all(bb, sl):
        def f(p, carry):
            for cp in page_copies(bb, sl, p):
                cp.wait()
            return carry
        lax.fori_loop(0, n_pages, f, 0)

    @pl.when(b == 0)
    def _():
        start_all(0, 0)

    @pl.when(b + 1 < nb)
    def _():
        start_all(b + 1, 1 - slot)

    tail = lp - past
    pad = jnp.zeros((tail - t_new, LANES), F32)
    kibuf[slot, pl.ds(past, t_new), :] = kin_ref[0]
    kibuf[slot, pl.ds(past + t_new, tail - t_new), :] = pad[:, :IDX_DIM]
    kbuf[slot, pl.ds(past, t_new), :] = kn_ref[0]
    kbuf[slot, pl.ds(past + t_new, tail - t_new), :] = pad
    vbuf[slot, pl.ds(past, t_new), :] = vn_ref[0]
    vbuf[slot, pl.ds(past + t_new, tail - t_new), :] = pad

    wait_all(b, slot)

    nt = (((1,), (1,)), ((), ()))
    s_idx = lax.dot_general(qi_ref[0].astype(BF16), kibuf[slot].astype(BF16), nt,
                            preferred_element_type=F32)
    wi = wi_ref[0] * (IDX_DIM ** -0.5)
    score = None
    for h in range(IDX_HEADS):
        term = jnp.maximum(s_idx[h * t_new:(h + 1) * t_new, :], 0.0) * wi[:, h:h + 1]
        score = term if score is None else score + term
    q_pos = past + lax.broadcasted_iota(jnp.int32, (t_new, lp), 0)
    k_pos = lax.broadcasted_iota(jnp.int32, (t_new, lp), 1)
    bias = _topk_bias(jnp.where(k_pos <= q_pos, score, NEG_INF), ksel)

    qb = (q_ref[0] * (HEAD_DIM ** -0.5)).astype(BF16)
    s = lax.dot_general(qb, kbuf[slot].astype(BF16), nt, preferred_element_type=F32)
    s = s + jnp.concatenate([bias] * ATT_HEADS, axis=0)
    m = jnp.max(s, axis=-1, keepdims=True)
    p = jnp.exp(s - m)
    l = jnp.sum(p, axis=-1, keepdims=True)
    o_ref[0] = jnp.dot(p.astype(BF16), vbuf[slot].astype(BF16), preferred_element_type=F32) / l


def _dsa_sample(page_table, qi_ht, wi, q_ht, ki_new, k_new, v_new, cki, ck, cv):
    B, n_pages = page_table.shape
    page = cki.shape[1]
    t_new = ki_new.shape[1]
    past = n_pages * page
    lp = past + LANES
    ksel = min(TOPK_MAX, (past + t_new) // 4)
    rows = ATT_HEADS * t_new
    grid_spec = pltpu.PrefetchScalarGridSpec(
        num_scalar_prefetch=1,
        grid=(B,),
        in_specs=[
            pl.BlockSpec((1, rows, IDX_DIM), lambda b, pt: (b, 0, 0)),
            pl.BlockSpec((1, t_new, LANES), lambda b, pt: (b, 0, 0)),
            pl.BlockSpec((1, rows, LANES), lambda b, pt: (b, 0, 0)),
            pl.BlockSpec((1, t_new, IDX_DIM), lambda b, pt: (b, 0, 0)),
            pl.BlockSpec((1, t_new, LANES), lambda b, pt: (b, 0, 0)),
            pl.BlockSpec((1, t_new, LANES), lambda b, pt: (b, 0, 0)),
            pl.BlockSpec(memory_space=pl.ANY),
            pl.BlockSpec(memory_space=pl.ANY),
            pl.BlockSpec(memory_space=pl.ANY),
        ],
        out_specs=pl.BlockSpec((1, rows, LANES), lambda b, pt: (b, 0, 0)),
        scratch_shapes=[
            pltpu.VMEM((2, lp, IDX_DIM), F32),
            pltpu.VMEM((2, lp, LANES), F32),
            pltpu.VMEM((2, lp, LANES), F32),
            pltpu.SemaphoreType.DMA((2, 3)),
        ],
    )
    return pl.pallas_call(
        functools.partial(_dsa_sample_body, n_pages=n_pages, page=page, t_new=t_new, ksel=ksel),
        grid_spec=grid_spec,
        out_shape=jax.ShapeDtypeStruct((B, rows, LANES), F32),
        compiler_params=_cparams(("arbitrary",)),
        name="dsa_sample",
    )(page_table, qi_ht, wi, q_ht, ki_new, k_new, v_new, cki, ck, cv)


def _softplus(x):
    return jnp.maximum(x, 0.0) + jnp.log(1.0 + jnp.exp(-jnp.abs(x)))


def _gdn_body(qkv_ref, z_ref, sm_ref, gt_ref, cw_ref, alr_ref, dtr_ref, alc_ref, dtc_ref, nw_ref, c0_ref, s0_ref,
              o_ref, sT_ref, cT_ref, xbuf, qs, ks, vs, bcol, gcol, s_scr, *, ti, tt, chunk):
    i = pl.program_id(1)
    n_i = pl.num_programs(1)
    nq = GDN_HEADS * GDN_DK
    halo = 8

    @pl.when(i == 0)
    def _():
        s_scr[...] = s0_ref[0]
        xbuf[halo - (CONV_K - 1):halo, :] = c0_ref[0]

    xbuf[halo:halo + ti, :] = qkv_ref[...]
    new_tail = xbuf[halo + ti - (CONV_K - 1):halo + ti, :]
    cw = cw_ref[...]
    y = None
    for j in range(CONV_K):
        term = xbuf[halo - (CONV_K - 1) + j:halo - (CONV_K - 1) + j + ti, :] * cw[j:j + 1, :]
        y = term if y is None else y + term
    y = _silu(y)
    xbuf[halo - (CONV_K - 1):halo, :] = new_tail

    @pl.when(i == n_i - 1)
    def _():
        cT_ref[0] = new_tail

    if ti < tt:
        zpad = jnp.zeros((tt - ti, nq), F32)
        qs[ti:tt, :] = zpad
        ks[ti:tt, :] = zpad
        vs[ti:tt, :] = zpad
        bcol[ti:tt, :] = zpad[:, :LANES]
        gcol[ti:tt, :] = zpad[:, :LANES]
    for h in range(GDN_HEADS):
        qh = y[:, h * GDN_DK:(h + 1) * GDN_DK]
        kh = y[:, nq + h * GDN_DK:nq + (h + 1) * GDN_DK]
        qs[0:ti, h * GDN_DK:(h + 1) * GDN_DK] = qh * lax.rsqrt(jnp.sum(qh * qh, -1, keepdims=True) + NORM_EPS)
        ks[0:ti, h * GDN_DK:(h + 1) * GDN_DK] = kh * lax.rsqrt(jnp.sum(kh * kh, -1, keepdims=True) + NORM_EPS)
    vs[0:ti, :] = y[:, 2 * nq:]
    sm = sm_ref[...]
    bcol[0:ti, :] = jax.nn.sigmoid(sm)
    gcol[0:ti, :] = -jnp.exp(alr_ref[...]) * _softplus(sm + dtr_ref[...])

    ri = lax.broadcasted_iota(jnp.int32, (chunk, chunk), 0)
    ci = lax.broadcasted_iota(jnp.int32, (chunk, chunk), 1)
    causal = ri >= ci
    strict = ri > ci
    ltri = jnp.where(causal, 1.0, 0.0)
    utri = jnp.where(ri <= ci, 1.0, 0.0)
    col_live = lax.broadcasted_iota(jnp.int32, (8, chunk), 1) < ti
    n_levels = int(math.log2(chunk))
    nt = (((1,), (1,)), ((), ()))
    tn = (((0,), (0,)), ((), ()))

    def chunk_step(c, carry):
        r0 = pl.multiple_of(c * chunk, chunk)
        gcum_col = jnp.dot(ltri, gcol[pl.ds(r0, chunk), :], precision=HI, preferred_element_type=F32)
        g_rows = -jnp.exp(alc_ref[...]) * _softplus(gt_ref[c] + dtc_ref[...])
        g_rows = jnp.where(col_live, g_rows, 0.0)
        gcum_row = jnp.dot(g_rows, utri, precision=HI, preferred_element_type=F32)
        b_all = bcol[pl.ds(r0, chunk), :]
        for h in range(GDN_HEADS):
            hs = slice(h * GDN_DK, (h + 1) * GDN_DK)
            kc = ks[pl.ds(r0, chunk), hs]
            qc = qs[pl.ds(r0, chunk), hs] * (GDN_DK ** -0.5)
            vc = vs[pl.ds(r0, chunk), hs]
            beta = b_all[:, 8 + h:9 + h]
            gc = gcum_col[:, 12 + h:13 + h]
            gr = gcum_row[4 + h:5 + h, :]
            g_last = gc[chunk - 1:chunk, :]
            decay = jnp.exp(jnp.where(causal, gc - gr, NEG_INF))
            kb = kc * beta
            m = jnp.where(strict, lax.dot_general(kb, kc, nt, precision=HI, preferred_element_type=F32) * decay, 0.0)
            rhs = jnp.concatenate([vc * beta, kb * jnp.exp(gc)], axis=1)
            pw = [-m]
            for _ in range(n_levels - 1):
                pw.append(jnp.dot(pw[-1], pw[-1], precision=HI, preferred_element_type=F32))
            sol = rhs
            for a in reversed(pw):
                sol = sol + jnp.dot(a, sol, precision=HI, preferred_element_type=F32)
            u = sol[:, :GDN_DV]
            w = sol[:, GDN_DV:]
            qk = jnp.where(causal, lax.dot_general(qc, kc, nt, precision=HI, preferred_element_type=F32) * decay, 0.0)
            s_prev = s_scr[h]
            v_new = u - jnp.dot(w, s_prev, precision=HI, preferred_element_type=F32)
            o = (jnp.dot(qc * jnp.exp(gc), s_prev, precision=HI, preferred_element_type=F32)
                 + jnp.dot(qk, v_new, precision=HI, preferred_element_type=F32))
            s_scr[h] = (s_prev * jnp.exp(g_last)
                        + lax.dot_general(kc * jnp.exp(g_last - gc), v_new, tn, precision=HI,
                                          preferred_element_type=F32))
            o = o * lax.rsqrt(jnp.mean(o * o, -1, keepdims=True) + NORM_EPS) * nw_ref[...]
            if ti < tt:
                o_ref[:, hs] = o[:ti] * _silu(z_ref[:, hs])
            else:
                o_ref[pl.ds(r0, chunk), hs] = o * _silu(z_ref[pl.ds(r0, chunk), hs])
        return carry

    lax.fori_loop(0, tt // chunk, chunk_step, 0)

    @pl.when(i == n_i - 1)
    def _():
        sT_ref[0] = s_scr[...]


def _gdn(qkv, z, small, gates_t, conv_w, a_log, dt_bias, norm_w, conv0, s0, B, T):
    chunk = GDN_CHUNK
    ti = min(512, T)
    tt = max(ti, chunk)
    n_i = T // ti
    ncb = tt // chunk
    pad = (0, LANES - 16)
    alr = jnp.pad(jnp.concatenate([jnp.zeros((12,), F32), a_log]), pad).reshape(1, LANES)
    dtr = jnp.pad(jnp.concatenate([jnp.zeros((12,), F32), dt_bias]), pad).reshape(1, LANES)
    alc = jnp.concatenate([jnp.zeros((4,), F32), a_log]).reshape(8, 1)
    dtc = jnp.concatenate([jnp.zeros((4,), F32), dt_bias]).reshape(8, 1)
    nw = norm_w.reshape(1, GDN_DV)
    C = GDN_CONV_DIM
    W = GDN_WIDTH

    def const(shape):
        return pl.BlockSpec(shape, lambda b, i: (0,) * len(shape))

    o, s_fin, c_fin = pl.pallas_call(
        functools.partial(_gdn_body, ti=ti, tt=tt, chunk=chunk),
        grid=(B, n_i),
        in_specs=[
            pl.BlockSpec((ti, C), lambda b, i: (b * n_i + i, 0)),
            pl.BlockSpec((ti, W), lambda b, i: (b * n_i + i, 0)),
            pl.BlockSpec((ti, LANES), lambda b, i: (b * n_i + i, 0)),
            pl.BlockSpec((ncb, 8, chunk), lambda b, i: (b * n_i + i, 0, 0)),
            const((CONV_K, C)), const((1, LANES)), const((1, LANES)), const((8, 1)), const((8, 1)),
            const((1, GDN_DV)),
            pl.BlockSpec((1, CONV_K - 1, C), lambda b, i: (b, 0, 0)),
            pl.BlockSpec((1, GDN_HEADS, GDN_DK, GDN_DV), lambda b, i: (b, 0, 0, 0)),
        ],
        out_specs=[
            pl.BlockSpec((ti, W), lambda b, i: (b * n_i + i, 0)),
            pl.BlockSpec((1, GDN_HEADS, GDN_DK, GDN_DV), lambda b, i: (b, 0, 0, 0)),
            pl.BlockSpec((1, CONV_K - 1, C), lambda b, i: (b, 0, 0)),
        ],
        out_shape=[
            jax.ShapeDtypeStruct((B * T, W), F32),
            jax.ShapeDtypeStruct((B, GDN_HEADS, GDN_DK, GDN_DV), F32),
            jax.ShapeDtypeStruct((B, CONV_K - 1, C), F32),
        ],
        scratch_shapes=[
            pltpu.VMEM((ti + 8, C), F32),
            pltpu.VMEM((tt, W), F32), pltpu.VMEM((tt, W), F32), pltpu.VMEM((tt, W), F32),
            pltpu.VMEM((tt, LANES), F32), pltpu.VMEM((tt, LANES), F32),
            pltpu.VMEM((GDN_HEADS, GDN_DK, GDN_DV), F32),
        ],
        compiler_params=_cparams(("parallel", "arbitrary")),
        name="gated_delta",
    )(qkv, z, small, gates_t, conv_w, alr, dtr, alc, dtc, nw, conv0, s0)
    return o, s_fin, c_fin


def _gates_transposed(small, B, T):
    chunk = GDN_CHUNK
    g = small[:, IDX_HEADS:IDX_HEADS + 2 * GDN_HEADS].reshape(B, T, 2 * GDN_HEADS)
    tpad = -(-T // chunk) * chunk
    g = jnp.pad(g, ((0, 0), (0, tpad - T), (0, 0)))
    return g.reshape(B * tpad // chunk, chunk, 2 * GDN_HEADS).transpose(0, 2, 1)


def _mixer_body(x_ref, att_ref, gdn_ref, woa_ref, wog_ref, g_ref, b_ref, o_ref):
    mix = (jnp.dot(att_ref[...].astype(BF16), woa_ref[...], preferred_element_type=F32)
           + jnp.dot(gdn_ref[...].astype(BF16), wog_ref[...], preferred_element_type=F32))
    o_ref[...] = _layer_norm_rows(DEEP_ALPHA * x_ref[...] + mix, g_ref[...], b_ref[...])


def _mixer_residual(x, att, gdn, wo_att, wo_gdn, g, b):
    M, D = x.shape
    tm = min(512, M)

    def rows(width):
        return pl.BlockSpec((tm, width), lambda i: (i, 0))

    def full(a):
        return pl.BlockSpec(a.shape, lambda i: (0, 0))

    return pl.pallas_call(
        _mixer_body,
        grid=(M // tm,),
        in_specs=[rows(D), rows(ATT_WIDTH), rows(GDN_WIDTH), full(wo_att), full(wo_gdn), full(g), full(b)],
        out_specs=rows(D),
        out_shape=jax.ShapeDtypeStruct((M, D), F32),
        compiler_params=_cparams(("parallel",)),
        name="mixer_residual",
    )(x, att, gdn, wo_att, wo_gdn, g, b)


def _rope_tables(pos):
    d_rot = HEAD_DIM // 4
    half = d_rot // 2
    inv_freq = ROPE_THETA ** (-jnp.arange(half, dtype=F32) / half)
    ang = pos.astype(F32)[:, None] * inv_freq[None, :]
    cos = jnp.cos(ang)
    sin = jnp.sin(ang)
    n = pos.shape[0]
    ones = jnp.ones((n, HEAD_DIM - d_rot), F32)
    zeros = jnp.zeros((n, HEAD_DIM - d_rot), F32)
    zh = jnp.zeros((n, half), F32)
    c = jnp.concatenate([cos, cos, ones], axis=1)
    sa = jnp.concatenate([-sin, zh, zeros], axis=1)
    sb = jnp.concatenate([zh, sin, zeros], axis=1)
    rep = LANES // HEAD_DIM
    return jnp.tile(c, (1, rep)), jnp.tile(sa, (1, rep)), jnp.tile(sb, (1, rep))


def _split_w_in(w_in):
    pts = [0]
    for s in SPLITS:
        pts.append(pts[-1] + s)
    wq, wk, wv, wqi, wki, wwi, wqkv, wz, wb, wa = [w_in[:, pts[n]:pts[n + 1]] for n in range(len(SPLITS))]
    D = w_in.shape[0]
    wr = jnp.concatenate([wq, wqi, wk, wki, jnp.zeros((D, ROPE_W - 2 * ATT_WIDTH - LANES - IDX_DIM), F32)], axis=1)
    ws = jnp.concatenate([wwi, wb, wa, jnp.zeros((D, LANES - IDX_HEADS - 2 * GDN_HEADS), F32)], axis=1)
    return tuple(a.astype(BF16) for a in (wr, wv, ws, wqkv, wz))


def kernel(x_prompt, x_sample, cache_k, cache_v, cache_kidx, state_gdn, state_conv, page_table, ffn1_w_gate, ffn1_w_up, ffn1_w_down, ln1_g, ln1_b, w_in, conv_w, a_log, dt_bias, gdn_norm_w, w_o, ln2_g, ln2_b, ffn2_w_gate, ffn2_w_up, ffn2_w_down, ln3_g, ln3_b):
    B, S, D = x_prompt.shape
    Bs, T, _ = x_sample.shape
    n_pages = page_table.shape[1]
    page = cache_k.shape[2]
    past = n_pages * page
    l = 0
    xp = x_prompt.reshape(B * S, D)
    xs = x_sample.reshape(Bs * T, D)

    f1 = (ffn1_w_gate[l].astype(BF16), ffn1_w_up[l].astype(BF16), ffn1_w_down[l].astype(BF16), ln1_g, ln1_b)
    f2 = (ffn2_w_gate[l].astype(BF16), ffn2_w_up[l].astype(BF16), ffn2_w_down[l].astype(BF16), ln3_g, ln3_b)
    w_proj = _split_w_in(w_in[l])
    wo_att = w_o[l][:ATT_WIDTH].astype(BF16)
    wo_gdn = w_o[l][ATT_WIDTH:].astype(BF16)

    xp1 = _ffn_half_step(xp, *f1)
    tm_p = min(512, B * S)
    q, qi, k, ki, v, kt, kit, small, qkv, z = _mix_projection(xp1, w_proj, _rope_tables(jnp.arange(S)), S // tm_p)
    att = _dsa_prompt(q, qi, small, kt, kit, v, B, S)
    gdn, s_p, c_p = _gdn(qkv, z, small, _gates_transposed(small, B, S), conv_w[l], a_log[l], dt_bias[l],
                         gdn_norm_w[l], jnp.zeros((B, CONV_K - 1, GDN_CONV_DIM), F32),
                         jnp.zeros((B, GDN_HEADS, GDN_DK, GDN_DV), F32), B, S)
    xp2 = _mixer_residual(xp1, att, gdn, wo_att, wo_gdn, ln2_g, ln2_b)
    y_prompt = _ffn_half_step(xp2, *f2).reshape(B, S, D)
    k_prompt = k.reshape(1, B, S, N_KV_HEADS, HEAD_DIM)
    v_prompt = v.reshape(1, B, S, N_KV_HEADS, HEAD_DIM)
    kidx_prompt = ki.reshape(1, B, S, IDX_DIM)

    xs1 = _ffn_half_step(xs, *f1)
    pos_s = jnp.tile(past + jnp.arange(T), Bs)
    tm_s = min(512, Bs * T)
    q, qi, k, ki, v, _, _, small, qkv, z = _mix_projection(xs1, w_proj, _rope_tables(pos_s), Bs * T // tm_s)
    group = ATT_HEADS // N_KV_HEADS
    qh = q.reshape(Bs, T, N_KV_HEADS, group, HEAD_DIM).transpose(0, 2, 3, 1, 4)
    zq = jnp.zeros_like(qh[:, 0])
    q_ht = jnp.concatenate([jnp.concatenate([qh[:, 0], zq], -1), jnp.concatenate([zq, qh[:, 1]], -1)], axis=1)
    q_ht = q_ht.reshape(Bs, ATT_HEADS * T, LANES)
    qi_ht = qi.reshape(Bs, T, IDX_HEADS, IDX_DIM).transpose(0, 2, 1, 3).reshape(Bs, IDX_HEADS * T, IDX_DIM)
    o_ht = _dsa_sample(page_table, qi_ht, small.reshape(Bs, T, LANES), q_ht, ki.reshape(Bs, T, IDX_DIM),
                       k.reshape(Bs, T, LANES), v.reshape(Bs, T, LANES),
                       cache_kidx[l], cache_k[l].reshape(-1, page, LANES), cache_v[l].reshape(-1, page, LANES))
    o_ht = o_ht.reshape(Bs, N_KV_HEADS, group, T, N_KV_HEADS, HEAD_DIM)
    att = jnp.stack([o_ht[:, 0, :, :, 0], o_ht[:, 1, :, :, 1]], axis=1)
    att = att.transpose(0, 3, 1, 2, 4).reshape(Bs * T, ATT_WIDTH)
    gdn, s_s, c_s = _gdn(qkv, z, small, _gates_transposed(small, Bs, T), conv_w[l], a_log[l], dt_bias[l],
                         gdn_norm_w[l], state_conv[l], state_gdn[l], Bs, T)
    xs2 = _mixer_residual(xs1, att, gdn, wo_att, wo_gdn, ln2_g, ln2_b)
    y_sample = _ffn_half_step(xs2, *f2).reshape(Bs, T, D)

    return (y_prompt, y_sample,
            k_prompt, v_prompt, kidx_prompt, s_p[None], c_p[None],
            k.reshape(1, Bs, T, N_KV_HEADS, HEAD_DIM), v.reshape(1, Bs, T, N_KV_HEADS, HEAD_DIM),
            ki.reshape(1, Bs, T, IDX_DIM), s_s[None], c_s[None])
```

```python
import functools
import math

import jax
import jax.numpy as jnp
from jax import lax
from jax.experimental import pallas as pl
from jax.experimental.pallas import tpu as pltpu

F32 = jnp.float32
BF16 = jnp.bfloat16
HI = lax.Precision.HIGHEST

LANES = 128
VMEM_LIMIT = 56 * 1024 * 1024

D_MODEL = 1024
DEPTH = 1
ATT_HEADS = 8
N_KV_HEADS = 2
HEAD_DIM = 64
ATT_WIDTH = ATT_HEADS * HEAD_DIM
ROPE_THETA = 500000.0
IDX_HEADS = 8
IDX_DIM = 64
TOPK_MAX = 256
GDN_HEADS = 4
GDN_DK = 128
GDN_DV = 128
GDN_WIDTH = GDN_HEADS * GDN_DV
GDN_CONV_DIM = 2 * GDN_HEADS * GDN_DK + GDN_WIDTH
CONV_K = 4
GDN_CHUNK = 64
LN_EPS = 1e-5
NORM_EPS = 1e-6
DEEP_ALPHA = (2 * DEPTH) ** 0.25
SPLITS = (ATT_WIDTH, N_KV_HEADS * HEAD_DIM, N_KV_HEADS * HEAD_DIM, IDX_HEADS * IDX_DIM, IDX_DIM, IDX_HEADS,
          GDN_CONV_DIM, GDN_WIDTH, GDN_HEADS, GDN_HEADS)
ROPE_W = 1280
N_BISECT = 20

NEG_INF = float("-inf")
POS_INF = float("inf")
NN = (((1,), (0,)), ((), ()))
NT = (((1,), (1,)), ((), ()))
TN = (((0,), (0,)), ((), ()))
ROW_TILE = 512
QUERY_TILE = 128
IDX_KEY_CHUNK = 256
CAUSAL_CLASSES = 4
GDN_MM_GRAM = "bf16"
GDN_MM_SOLVE = "x3"
GDN_MM_STATE = "bf16"


def _cparams(sem):
    return pltpu.CompilerParams(dimension_semantics=sem, vmem_limit_bytes=VMEM_LIMIT)


def _layer_norm_rows(y, g, b):
    mu = jnp.mean(y, axis=-1, keepdims=True)
    d = y - mu
    var = jnp.mean(d * d, axis=-1, keepdims=True)
    return d * lax.rsqrt(var + LN_EPS) * g + b


def _silu(x):
    return x * jax.nn.sigmoid(x)


def _softplus(x):
    return jnp.maximum(x, 0.0) + jnp.log(1.0 + jnp.exp(-jnp.abs(x)))


def _split_bf16(x):
    hi = x.astype(BF16)
    return hi, (x - hi.astype(F32)).astype(BF16)


def _mm(a, b, dims, mode):
    if mode == "f32":
        return lax.dot_general(a, b, dims, precision=HI, preferred_element_type=F32)
    dot = functools.partial(lax.dot_general, dimension_numbers=dims, preferred_element_type=F32)
    if mode == "bf16":
        return dot(a.astype(BF16), b.astype(BF16))
    ah, al = _split_bf16(a)
    bh, bl = _split_bf16(b)
    return dot(ah, bh) + (dot(ah, bl) + dot(al, bh))


def _ffn_body(x_ref, wg_ref, wu_ref, wd_ref, g_ref, b_ref, o_ref, acc_ref):
    j = pl.program_id(1)
    xb = x_ref[...].astype(BF16)
    hg = jnp.dot(xb, wg_ref[...], preferred_element_type=F32)
    hu = jnp.dot(xb, wu_ref[...], preferred_element_type=F32)
    h = _silu(hg) * hu
    part = jnp.dot(h.astype(BF16), wd_ref[...], preferred_element_type=F32)

    @pl.when(j == 0)
    def _():
        acc_ref[...] = part

    @pl.when(j > 0)
    def _():
        acc_ref[...] += part

    @pl.when(j == pl.num_programs(1) - 1)
    def _():
        y = DEEP_ALPHA * x_ref[...] + 0.5 * acc_ref[...]
        o_ref[...] = _layer_norm_rows(y, g_ref[...], b_ref[...])


def _ffn_half_step(x, wg, wu, wd, g, b):
    M, D = x.shape
    FF = wg.shape[1]
    tm = min(ROW_TILE, M)
    tf = FF // 2 if (FF // 2) % LANES == 0 else FF
    return pl.pallas_call(
        _ffn_body,
        grid=(M // tm, FF // tf),
        in_specs=[
            pl.BlockSpec((tm, D), lambda i, j: (i, 0)),
            pl.BlockSpec((D, tf), lambda i, j: (0, j)),
            pl.BlockSpec((D, tf), lambda i, j: (0, j)),
            pl.BlockSpec((tf, D), lambda i, j: (j, 0)),
            pl.BlockSpec((1, D), lambda i, j: (0, 0)),
            pl.BlockSpec((1, D), lambda i, j: (0, 0)),
        ],
        out_specs=pl.BlockSpec((tm, D), lambda i, j: (i, 0)),
        out_shape=jax.ShapeDtypeStruct((M, D), F32),
        scratch_shapes=[pltpu.VMEM((tm, D), F32)],
        compiler_params=_cparams(("parallel", "arbitrary")),
        name="ffn_half_step",
    )(x, wg, wu, wd, g, b)


def _proj_body(x_ref, wr_ref, wv_ref, ws_ref, wqkv_ref, wz_ref, cos_ref, sa_ref, sb_ref,
               q_ref, qi_ref, k_ref, ki_ref, v_ref, small_ref, qkv_ref, z_ref, *, transposed):
    xb = x_ref[...].astype(BF16)
    r = jnp.dot(xb, wr_ref[...], preferred_element_type=F32)
    c = cos_ref[...]
    sa = sa_ref[...]
    sb = sb_ref[...]

    def rope(slab):
        return slab * c + pltpu.roll(slab, LANES - 8, 1) * sa + pltpu.roll(slab, 8, 1) * sb

    for s in range(ATT_WIDTH // LANES):
        q_ref[:, s * LANES:(s + 1) * LANES] = rope(r[:, s * LANES:(s + 1) * LANES])
    off = ATT_WIDTH
    for s in range(IDX_HEADS * IDX_DIM // LANES):
        qi_ref[:, s * LANES:(s + 1) * LANES] = rope(r[:, off + s * LANES:off + (s + 1) * LANES])
    off += IDX_HEADS * IDX_DIM
    kr = rope(r[:, off:off + LANES])
    kir = rope(r[:, off + LANES:off + 2 * LANES])
    v = jnp.dot(xb, wv_ref[...], preferred_element_type=F32)
    if transposed:
        k_ref[0] = kr.T
        ki_ref[0] = kir.T[:IDX_DIM, :]
        v_ref[0] = v.T
    else:
        k_ref[...] = kr
        ki_ref[...] = kir[:, :IDX_DIM]
        v_ref[...] = v
    sm = jnp.dot(xb, ws_ref[...], preferred_element_type=F32)
    lane = lax.broadcasted_iota(jnp.int32, sm.shape, 1)
    small_ref[...] = jnp.where(lane < IDX_HEADS, sm * (IDX_HEADS ** -0.5), sm)
    qkv_ref[...] = jnp.dot(xb, wqkv_ref[...], preferred_element_type=F32)
    z_ref[...] = jnp.dot(xb, wz_ref[...], preferred_element_type=F32)


def _mix_projection(x, w, tabs, n_tab_blocks, seq=None):
    M, D = x.shape
    tm = min(ROW_TILE, M)
    wr, wv, ws, wqkv, wz = w
    cos_t, sa_t, sb_t = tabs
    kvw = N_KV_HEADS * HEAD_DIM

    def full(a):
        return pl.BlockSpec(a.shape, lambda i: (0, 0))

    def rows(width):
        return pl.BlockSpec((tm, width), lambda i: (i, 0))

    tab_spec = pl.BlockSpec((tm, LANES), lambda i: (i % n_tab_blocks, 0))
    if seq is None:
        kv_shapes = [jax.ShapeDtypeStruct((M, kvw), F32), jax.ShapeDtypeStruct((M, IDX_DIM), F32),
                     jax.ShapeDtypeStruct((M, kvw), F32)]
        kv_specs = [rows(kvw), rows(IDX_DIM), rows(kvw)]
    else:
        nt = seq // tm
        nb = M // seq

        def tr(width):
            return pl.BlockSpec((1, width, tm), lambda i: (i // nt, 0, i % nt))

        kv_shapes = [jax.ShapeDtypeStruct((nb, kvw, seq), F32), jax.ShapeDtypeStruct((nb, IDX_DIM, seq), F32),
                     jax.ShapeDtypeStruct((nb, kvw, seq), F32)]
        kv_specs = [tr(kvw), tr(IDX_DIM), tr(kvw)]
    out_shapes = [
        jax.ShapeDtypeStruct((M, ATT_WIDTH), F32),
        jax.ShapeDtypeStruct((M, IDX_HEADS * IDX_DIM), F32),
    ] + kv_shapes + [
        jax.ShapeDtypeStruct((M, LANES), F32),
        jax.ShapeDtypeStruct((M, GDN_CONV_DIM), F32),
        jax.ShapeDtypeStruct((M, GDN_WIDTH), F32),
    ]
    out_specs = [rows(ATT_WIDTH), rows(IDX_HEADS * IDX_DIM)] + kv_specs + [
        rows(LANES), rows(GDN_CONV_DIM), rows(GDN_WIDTH)]
    return pl.pallas_call(
        functools.partial(_proj_body, transposed=seq is not None),
        grid=(M // tm,),
        in_specs=[rows(D), full(wr), full(wv), full(ws), full(wqkv), full(wz), tab_spec, tab_spec, tab_spec],
        out_specs=out_specs,
        out_shape=out_shapes,
        compiler_params=_cparams(("parallel",)),
        name="mix_projection",
    )(x, wr, wv, ws, wqkv, wz, cos_t, sa_t, sb_t)


def _count_ge(xm, thr):
    return jnp.sum(jnp.where(xm >= thr, 1.0, 0.0), axis=-1, keepdims=True)


def _prefix_count(eq_f32):
    R, L = eq_f32.shape
    ri = lax.broadcasted_iota(jnp.int32, (LANES, LANES), 0)
    ci = lax.broadcasted_iota(jnp.int32, (LANES, LANES), 1)
    tri = jnp.where(ri <= ci, 1.0, 0.0).astype(BF16)
    offset = jnp.zeros((R, 1), F32)
    pieces = []
    for blk in range(L // LANES):
        e = eq_f32[:, blk * LANES:(blk + 1) * LANES].astype(BF16)
        loc = jnp.dot(e, tri, preferred_element_type=F32)
        pieces.append(loc + offset)
        offset = offset + loc[:, LANES - 1:LANES]
    return jnp.concatenate(pieces, axis=1)


def _topk_bias(xm, k):
    kf = float(k)
    rowmax = jnp.max(xm, axis=-1, keepdims=True)
    rowmin = jnp.min(jnp.where(xm == NEG_INF, POS_INF, xm), axis=-1, keepdims=True)
    lo0 = rowmin
    hi0 = rowmax + jnp.abs(rowmax) * (2.0 ** -10) + 1.0

    def bisect(_, c):
        lo, hi = c
        mid = 0.5 * (lo + hi)
        ge = _count_ge(xm, mid) >= kf
        return jnp.where(ge, mid, lo), jnp.where(ge, hi, mid)

    lo, hi = lax.fori_loop(0, N_BISECT, bisect, (lo0, hi0))
    c_lo = _count_ge(xm, lo)
    pending = jnp.where(c_lo > kf, 1.0, 0.0)

    def cond(c):
        return jnp.max(c[3]) > 0.0

    def body(c):
        lo, hi, c_lo, pending = c
        t = jnp.max(jnp.where(xm < hi, xm, NEG_INF), axis=-1, keepdims=True)
        ct = _count_ge(xm, t)
        hit = ct >= kf
        live = pending > 0.0
        upd = jnp.logical_and(live, hit)
        lo = jnp.where(upd, t, lo)
        c_lo = jnp.where(upd, ct, c_lo)
        hi = jnp.where(jnp.logical_and(live, jnp.logical_not(hit)), t, hi)
        pending = jnp.where(hit, 0.0, pending)
        return lo, hi, c_lo, pending

    lo, hi, c_lo, pending = lax.while_loop(cond, body, (lo, hi, c_lo, pending))

    def plain():
        return jnp.where(xm >= lo, 0.0, NEG_INF)

    def with_ties():
        gt = xm > lo
        eq = xm == lo
        neg_zero = jnp.logical_and(xm == 0.0, 1.0 / xm < 0.0)
        eq_hi = jnp.logical_and(eq, jnp.logical_not(neg_zero))
        eq_lo = jnp.logical_and(eq, neg_zero)
        room = kf - jnp.sum(jnp.where(gt, 1.0, 0.0), axis=-1, keepdims=True)
        rank_hi = _prefix_count(jnp.where(eq_hi, 1.0, 0.0))
        rank_lo = _prefix_count(jnp.where(eq_lo, 1.0, 0.0)) + rank_hi[:, -1:]
        keep = jnp.logical_or(gt, jnp.logical_or(jnp.logical_and(eq_hi, rank_hi <= room),
                                                 jnp.logical_and(eq_lo, rank_lo <= room)))
        return jnp.where(keep, 0.0, NEG_INF)

    return lax.cond(jnp.max(c_lo) > kf, with_ties, plain)


def _indexer_scores(qi_bf, kit_ref, wi, sk):
    ws = wi * (IDX_DIM ** -0.5)
    pieces = []
    for kc in range(sk // IDX_KEY_CHUNK):
        kit = kit_ref[0, :, kc * IDX_KEY_CHUNK:(kc + 1) * IDX_KEY_CHUNK].astype(BF16)
        acc = None
        for h in range(IDX_HEADS):
            s = jnp.dot(qi_bf[:, h * IDX_DIM:(h + 1) * IDX_DIM], kit, preferred_element_type=F32)
            term = jnp.maximum(s, 0.0) * ws[:, h:h + 1]
            acc = term if acc is None else acc + term
        pieces.append(acc)
    return jnp.concatenate(pieces, axis=1)


def _dsa_prompt_block(q_ref, qi_ref, sm_ref, kt_ref, kit_ref, vt_ref, o_ref, i, *, tq, ksel, sk):
    q_pos = i * tq + lax.broadcasted_iota(jnp.int32, (tq, sk), 0)
    k_pos = lax.broadcasted_iota(jnp.int32, (tq, sk), 1)
    score = _indexer_scores(qi_ref[...].astype(BF16), kit_ref, sm_ref[...], sk)
    bias = _topk_bias(jnp.where(k_pos <= q_pos, score, NEG_INF), ksel)

    kt = kt_ref[0, :, :sk].astype(BF16)
    vt = vt_ref[0, :, :sk].astype(BF16)
    qb = (q_ref[...] * (HEAD_DIM ** -0.5)).astype(BF16)
    group = ATT_HEADS // N_KV_HEADS
    lane = lax.broadcasted_iota(jnp.int32, (tq, LANES), 1)
    outs = []
    for h in range(ATT_HEADS):
        kv = h // group
        s = jnp.dot(qb[:, h * HEAD_DIM:(h + 1) * HEAD_DIM], kt[kv * HEAD_DIM:(kv + 1) * HEAD_DIM, :],
                    preferred_element_type=F32) + bias
        m = jnp.max(s, axis=-1, keepdims=True)
        p = jnp.exp(s - m)
        l = jnp.sum(p, axis=-1, keepdims=True)
        outs.append(lax.dot_general(p.astype(BF16), vt, NT, preferred_element_type=F32) / l)
    for pair in range(ATT_HEADS // 2):
        h0, h1 = 2 * pair, 2 * pair + 1
        kv = h0 // group
        a, b = outs[h0], outs[h1]
        if kv == 0:
            b = pltpu.roll(b, HEAD_DIM, 1)
        else:
            a = pltpu.roll(a, HEAD_DIM, 1)
        o_ref[:, pair * LANES:(pair + 1) * LANES] = jnp.where(lane < HEAD_DIM, a, b)


def _dsa_prompt_body(q_ref, qi_ref, sm_ref, kt_ref, kit_ref, vt_ref, o_ref, *, tq, ksel, n_classes):
    i = pl.program_id(1)
    per = kt_ref.shape[2] // tq // n_classes
    for c in range(n_classes):
        @pl.when(jnp.logical_and(i >= c * per, i < (c + 1) * per))
        def _(c=c):
            _dsa_prompt_block(q_ref, qi_ref, sm_ref, kt_ref, kit_ref, vt_ref, o_ref, i,
                              tq=tq, ksel=ksel, sk=(c + 1) * per * tq)


def _dsa_prompt(q, qi, small, kt, kit, vt):
    B, kvw, S = kt.shape
    tq = QUERY_TILE
    nq = S // tq
    ksel = min(TOPK_MAX, S // 4)
    n_classes = CAUSAL_CLASSES if nq % CAUSAL_CLASSES == 0 and (nq // CAUSAL_CLASSES * tq) % IDX_KEY_CHUNK == 0 else 1
    return pl.pallas_call(
        functools.partial(_dsa_prompt_body, tq=tq, ksel=ksel, n_classes=n_classes),
        grid=(B, nq),
        in_specs=[
            pl.BlockSpec((tq, ATT_WIDTH), lambda b, i: (b * nq + i, 0)),
            pl.BlockSpec((tq, IDX_HEADS * IDX_DIM), lambda b, i: (b * nq + i, 0)),
            pl.BlockSpec((tq, LANES), lambda b, i: (b * nq + i, 0)),
            pl.BlockSpec((1, kvw, S), lambda b, i: (b, 0, 0)),
            pl.BlockSpec((1, IDX_DIM, S), lambda b, i: (b, 0, 0)),
            pl.BlockSpec((1, kvw, S), lambda b, i: (b, 0, 0)),
        ],
        out_specs=pl.BlockSpec((tq, ATT_WIDTH), lambda b, i: (b * nq + i, 0)),
        out_shape=jax.ShapeDtypeStruct((B * S, ATT_WIDTH), F32),
        compiler_params=_cparams(("parallel", "arbitrary")),
        name="dsa_prompt",
    )(q, qi, small, kt, kit, vt)


def _dsa_sample_body(pt_ref, qi_ref, wi_ref, q_ref, kin_ref, kn_ref, vn_ref, cki_hbm, ck_hbm, cv_hbm, o_ref,
                     kibuf, kbuf, vbuf, sem, *, n_pages, page, t_new, ksel):
    b = pl.program_id(0)
    nb = pl.num_programs(0)
    slot = lax.rem(b, 2)
    past = n_pages * page
    lp = kibuf.shape[2]

    def page_copies(bb, sl, p):
        pg = pt_ref[bb, p]
        cols = pl.ds(pl.multiple_of(p * page, page), page)
        return (pltpu.make_async_copy(cki_hbm.at[pg], kibuf.at[sl, :, cols], sem.at[sl, 0]),
                pltpu.make_async_copy(ck_hbm.at[pg], kbuf.at[sl, :, cols], sem.at[sl, 1]),
                pltpu.make_async_copy(cv_hbm.at[pg], vbuf.at[sl, :, cols], sem.at[sl, 2]))

    def start_all(bb, sl):
        def f(p, carry):
            for cp in page_copies(bb, sl, p):
                cp.start()
            return carry
        lax.fori_loop(0, n_pages, f, 0)

    def wait_all(bb, sl):
        def f(p, carry):
            for cp in page_copies(bb, sl, p):
                cp.wait()
            return carry
        lax.fori_loop(0, n_pages, f, 0)

    @pl.when(b == 0)
    def _():
        start_all(0, 0)

    @pl.when(b + 1 < nb)
    def _():
        start_all(b + 1, 1 - slot)

    kibuf[slot, :, past:lp] = kin_ref[0]
    kbuf[slot, :, past:lp] = kn_ref[0]
    vbuf[slot, :, past:lp] = vn_ref[0]

    wait_all(b, slot)

    s_idx = jnp.dot(qi_ref[0].astype(BF16), kibuf[slot].astype(BF16),
                    preferred_element_type=F32)
    wi = wi_ref[0] * (IDX_DIM ** -0.5)
    score = None
    for h in range(IDX_HEADS):
        term = jnp.maximum(s_idx[h * t_new:(h + 1) * t_new, :], 0.0) * wi[:, h:h + 1]
        score = term if score is None else score + term
    q_pos = past + lax.broadcasted_iota(jnp.int32, (t_new, lp), 0)
    k_pos = lax.broadcasted_iota(jnp.int32, (t_new, lp), 1)
    bias = _topk_bias(jnp.where(k_pos <= q_pos, score, NEG_INF), ksel)

    qb = (q_ref[0] * (HEAD_DIM ** -0.5)).astype(BF16)
    s = jnp.dot(qb, kbuf[slot].astype(BF16), preferred_element_type=F32)
    s = s + jnp.concatenate([bias] * ATT_HEADS, axis=0)
    m = jnp.max(s, axis=-1, keepdims=True)
    p = jnp.exp(s - m)
    l = jnp.sum(p, axis=-1, keepdims=True)
    o_ref[0] = lax.dot_general(p.astype(BF16), vbuf[slot].astype(BF16), NT, preferred_element_type=F32) / l


def _dsa_sample(page_table, qi_ht, wi, q_ht, kit_new, kt_new, vt_new, cki_t, ck_t, cv_t):
    B, n_pages = page_table.shape
    page = cki_t.shape[2]
    t_new = wi.shape[1]
    past = n_pages * page
    lp = past + LANES
    ksel = min(TOPK_MAX, (past + t_new) // 4)
    rows = ATT_HEADS * t_new
    kvw = N_KV_HEADS * HEAD_DIM
    grid_spec = pltpu.PrefetchScalarGridSpec(
        num_scalar_prefetch=1,
        grid=(B,),
        in_specs=[
            pl.BlockSpec((1, rows, IDX_DIM), lambda b, pt: (b, 0, 0)),
            pl.BlockSpec((1, t_new, LANES), lambda b, pt: (b, 0, 0)),
            pl.BlockSpec((1, rows, LANES), lambda b, pt: (b, 0, 0)),
            pl.BlockSpec((1, IDX_DIM, LANES), lambda b, pt: (b, 0, 0)),
            pl.BlockSpec((1, kvw, LANES), lambda b, pt: (b, 0, 0)),
            pl.BlockSpec((1, kvw, LANES), lambda b, pt: (b, 0, 0)),
            pl.BlockSpec(memory_space=pl.ANY),
            pl.BlockSpec(memory_space=pl.ANY),
            pl.BlockSpec(memory_space=pl.ANY),
        ],
        out_specs=pl.BlockSpec((1, rows, LANES), lambda b, pt: (b, 0, 0)),
        scratch_shapes=[
            pltpu.VMEM((2, IDX_DIM, lp), F32),
            pltpu.VMEM((2, kvw, lp), F32),
            pltpu.VMEM((2, kvw, lp), F32),
            pltpu.SemaphoreType.DMA((2, 3)),
        ],
    )
    return pl.pallas_call(
        functools.partial(_dsa_sample_body, n_pages=n_pages, page=page, t_new=t_new, ksel=ksel),
        grid_spec=grid_spec,
        out_shape=jax.ShapeDtypeStruct((B, rows, LANES), F32),
        compiler_params=_cparams(("arbitrary",)),
        name="dsa_sample",
    )(page_table, qi_ht, wi, q_ht, kit_new, kt_new, vt_new, cki_t, ck_t, cv_t)


def _gdn_body(qkv_ref, z_ref, sm_ref, gt_ref, cw_ref, alr_ref, dtr_ref, alc_ref, dtc_ref, nw_ref, c0_ref, s0_ref,
              o_ref, sT_ref, cT_ref, xbuf, qs, ks, vs, bcol, gcol, gcc, gcr, us, ws, qes, kds, qks,
              s0_scr, s1_scr, s2_scr, s3_scr, *, ti, tt, chunk):
    i = pl.program_id(1)
    n_i = pl.num_programs(1)
    nq = GDN_HEADS * GDN_DK
    ncb = tt // chunk
    halo = 8
    s_scr = (s0_scr, s1_scr, s2_scr, s3_scr)

    @pl.when(i == 0)
    def _():
        for h in range(GDN_HEADS):
            s_scr[h][...] = s0_ref[0, h]
        xbuf[halo - (CONV_K - 1):halo, :] = c0_ref[0]

    xbuf[halo:halo + ti, :] = qkv_ref[...]
    new_tail = xbuf[halo + ti - (CONV_K - 1):halo + ti, :]
    cw = cw_ref[...]
    y = None
    for j in range(CONV_K):
        term = xbuf[halo - (CONV_K - 1) + j:halo - (CONV_K - 1) + j + ti, :] * cw[j:j + 1, :]
        y = term if y is None else y + term
    y = _silu(y)
    xbuf[halo - (CONV_K - 1):halo, :] = new_tail

    @pl.when(i == n_i - 1)
    def _():
        cT_ref[0] = new_tail

    if ti < tt:
        zpad = jnp.zeros((tt - ti, nq), F32)
        qs[ti:tt, :] = zpad
        ks[ti:tt, :] = zpad
        vs[ti:tt, :] = zpad
        bcol[ti:tt, :] = zpad[:, :LANES]
        gcol[ti:tt, :] = zpad[:, :LANES]
    for h in range(GDN_HEADS):
        qh = y[:, h * GDN_DK:(h + 1) * GDN_DK]
        kh = y[:, nq + h * GDN_DK:nq + (h + 1) * GDN_DK]
        qs[0:ti, h * GDN_DK:(h + 1) * GDN_DK] = (qh * lax.rsqrt(jnp.sum(qh * qh, -1, keepdims=True) + NORM_EPS)
                                                 * (GDN_DK ** -0.5))
        ks[0:ti, h * GDN_DK:(h + 1) * GDN_DK] = kh * lax.rsqrt(jnp.sum(kh * kh, -1, keepdims=True) + NORM_EPS)
    vs[0:ti, :] = y[:, 2 * nq:]
    sm = sm_ref[...]
    bcol[0:ti, :] = jax.nn.sigmoid(sm)
    gcol[0:ti, :] = -jnp.exp(alr_ref[...]) * _softplus(sm + dtr_ref[...])

    ri = lax.broadcasted_iota(jnp.int32, (chunk, chunk), 0)
    ci = lax.broadcasted_iota(jnp.int32, (chunk, chunk), 1)
    causal = ri >= ci
    strict = ri > ci
    ltri = jnp.where(causal, 1.0, 0.0)
    utri = jnp.where(ri <= ci, 1.0, 0.0)
    col_live = lax.broadcasted_iota(jnp.int32, (8, chunk), 1) < ti
    n_levels = int(math.log2(chunk))

    for c in range(ncb):
        gcc[c * chunk:(c + 1) * chunk, :] = _mm(ltri, gcol[c * chunk:(c + 1) * chunk, :], NN, "f32")
        g_rows = -jnp.exp(alc_ref[...]) * _softplus(gt_ref[c] + dtc_ref[...])
        gcr[c] = _mm(jnp.where(col_live, g_rows, 0.0), utri, NN, "f32")

    def factor_step(c, carry):
        r0 = pl.multiple_of(c * chunk, chunk)
        rows = pl.ds(r0, chunk)
        gcum_col = gcc[rows, :]
        gcum_row = gcr[c]
        b_all = bcol[rows, :]
        heads = range(GDN_HEADS)
        hsl = [slice(h * GDN_DK, (h + 1) * GDN_DK) for h in heads]
        kc = [ks[rows, hsl[h]] for h in heads]
        qc = [qs[rows, hsl[h]] for h in heads]
        gc = [gcum_col[:, 12 + h:13 + h] for h in heads]
        eg = [jnp.exp(gc[h]) for h in heads]
        decay = [jnp.exp(jnp.where(causal, gc[h] - gcum_row[4 + h:5 + h, :], NEG_INF)) for h in heads]
        kb = [kc[h] * b_all[:, 8 + h:9 + h] for h in heads]
        gram = [_mm(jnp.concatenate([kb[h], qc[h]], axis=0), kc[h], NT, GDN_MM_GRAM) for h in heads]
        a = [-jnp.where(strict, gram[h][:chunk] * decay[h], 0.0) for h in heads]
        sol = [jnp.concatenate([vs[rows, hsl[h]] * b_all[:, 8 + h:9 + h], kb[h] * eg[h]], axis=1) for h in heads]
        for lvl in range(n_levels):
            sol = [sol[h] + _mm(a[h], sol[h], NN, GDN_MM_SOLVE) for h in heads]
            if lvl + 1 < n_levels:
                a = [_mm(a[h], a[h], NN, GDN_MM_SOLVE) for h in heads]
        for h in heads:
            qks[h, rows, :] = jnp.where(causal, gram[h][chunk:] * decay[h], 0.0)
            us[rows, hsl[h]] = sol[h][:, :GDN_DV]
            ws[rows, hsl[h]] = sol[h][:, GDN_DV:]
            qes[rows, hsl[h]] = qc[h] * eg[h]
            kds[rows, hsl[h]] = kc[h] * jnp.exp(gc[h][chunk - 1:chunk, :] - gc[h])
        return carry

    lax.fori_loop(0, ncb, factor_step, 0)

    def state_step(c, carry):
        r0 = pl.multiple_of(c * chunk, chunk)
        rows = pl.ds(r0, chunk)
        g_last_all = gcc[pl.ds(r0 + chunk - 1, 1), :]
        heads = range(GDN_HEADS)
        hsl = [slice(h * GDN_DK, (h + 1) * GDN_DK) for h in heads]
        s_prev = [s_scr[h][...] for h in heads]
        v_new = [us[rows, hsl[h]] - _mm(ws[rows, hsl[h]], s_prev[h], NN, GDN_MM_STATE) for h in heads]
        s_add = [_mm(kds[rows, hsl[h]], v_new[h], TN, GDN_MM_STATE) for h in heads]
        for h in heads:
            s_scr[h][...] = s_prev[h] * jnp.exp(g_last_all[:, 12 + h:13 + h]) + s_add[h]
        o = [_mm(qes[rows, hsl[h]], s_prev[h], NN, GDN_MM_STATE) + _mm(qks[h, rows, :], v_new[h], NN, GDN_MM_STATE)
             for h in heads]
        for h in heads:
            og = o[h] * lax.rsqrt(jnp.mean(o[h] * o[h], -1, keepdims=True) + NORM_EPS) * nw_ref[...]
            if ti < tt:
                o_ref[:, hsl[h]] = og[:ti] * _silu(z_ref[:, hsl[h]])
            else:
                o_ref[rows, hsl[h]] = og * _silu(z_ref[rows, hsl[h]])
        return carry

    lax.fori_loop(0, ncb, state_step, 0)

    @pl.when(i == n_i - 1)
    def _():
        for h in range(GDN_HEADS):
            sT_ref[0, h] = s_scr[h][...]


def _gdn(qkv, z, small, gates_t, conv_w, a_log, dt_bias, norm_w, conv0, s0, B, T):
    chunk = GDN_CHUNK
    ti = min(ROW_TILE, T)
    tt = max(ti, chunk)
    n_i = T // ti
    ncb = tt // chunk
    pad = (0, LANES - 16)
    alr = jnp.pad(jnp.concatenate([jnp.zeros((12,), F32), a_log]), pad).reshape(1, LANES)
    dtr = jnp.pad(jnp.concatenate([jnp.zeros((12,), F32), dt_bias]), pad).reshape(1, LANES)
    alc = jnp.concatenate([jnp.zeros((4,), F32), a_log]).reshape(8, 1)
    dtc = jnp.concatenate([jnp.zeros((4,), F32), dt_bias]).reshape(8, 1)
    nw = norm_w.reshape(1, GDN_DV)
    C = GDN_CONV_DIM
    W = GDN_WIDTH

    def const(shape):
        return pl.BlockSpec(shape, lambda b, i: (0,) * len(shape))

    o, s_fin, c_fin = pl.pallas_call(
        functools.partial(_gdn_body, ti=ti, tt=tt, chunk=chunk),
        grid=(B, n_i),
        in_specs=[
            pl.BlockSpec((ti, C), lambda b, i: (b * n_i + i, 0)),
            pl.BlockSpec((ti, W), lambda b, i: (b * n_i + i, 0)),
            pl.BlockSpec((ti, LANES), lambda b, i: (b * n_i + i, 0)),
            pl.BlockSpec((ncb, 8, chunk), lambda b, i: (b * n_i + i, 0, 0)),
            const((CONV_K, C)), const((1, LANES)), const((1, LANES)), const((8, 1)), const((8, 1)),
            const((1, GDN_DV)),
            pl.BlockSpec((1, CONV_K - 1, C), lambda b, i: (b, 0, 0)),
            pl.BlockSpec((1, GDN_HEADS, GDN_DK, GDN_DV), lambda b, i: (b, 0, 0, 0)),
        ],
        out_specs=[
            pl.BlockSpec((ti, W), lambda b, i: (b * n_i + i, 0)),
            pl.BlockSpec((1, GDN_HEADS, GDN_DK, GDN_DV), lambda b, i: (b, 0, 0, 0)),
            pl.BlockSpec((1, CONV_K - 1, C), lambda b, i: (b, 0, 0)),
        ],
        out_shape=[
            jax.ShapeDtypeStruct((B * T, W), F32),
            jax.ShapeDtypeStruct((B, GDN_HEADS, GDN_DK, GDN_DV), F32),
            jax.ShapeDtypeStruct((B, CONV_K - 1, C), F32),
        ],
        scratch_shapes=[
            pltpu.VMEM((ti + 8, C), F32),
            pltpu.VMEM((tt, W), F32), pltpu.VMEM((tt, W), F32), pltpu.VMEM((tt, W), F32),
            pltpu.VMEM((tt, LANES), F32), pltpu.VMEM((tt, LANES), F32),
            pltpu.VMEM((tt, LANES), F32), pltpu.VMEM((ncb, 8, chunk), F32),
            pltpu.VMEM((tt, W), F32), pltpu.VMEM((tt, W), F32),
            pltpu.VMEM((tt, W), F32), pltpu.VMEM((tt, W), F32),
            pltpu.VMEM((GDN_HEADS, tt, chunk), F32),
        ] + [pltpu.VMEM((GDN_DK, GDN_DV), F32)] * GDN_HEADS,
        compiler_params=_cparams(("parallel", "arbitrary")),
        name="gated_delta",
    )(qkv, z, small, gates_t, conv_w, alr, dtr, alc, dtc, nw, conv0, s0)
    return o, s_fin, c_fin


def _gates_transposed(small, B, T):
    chunk = GDN_CHUNK
    g = small[:, IDX_HEADS:IDX_HEADS + 2 * GDN_HEADS].reshape(B, T, 2 * GDN_HEADS)
    tpad = -(-T // chunk) * chunk
    g = jnp.pad(g, ((0, 0), (0, tpad - T), (0, 0)))
    return g.reshape(B * tpad // chunk, chunk, 2 * GDN_HEADS).transpose(0, 2, 1)


def _mixer_body(x_ref, att_ref, gdn_ref, woa_ref, wog_ref, g_ref, b_ref, o_ref):
    mix = (jnp.dot(att_ref[...].astype(BF16), woa_ref[...], preferred_element_type=F32)
           + jnp.dot(gdn_ref[...].astype(BF16), wog_ref[...], preferred_element_type=F32))
    o_ref[...] = _layer_norm_rows(DEEP_ALPHA * x_ref[...] + mix, g_ref[...], b_ref[...])


def _mixer_residual(x, att, gdn, wo_att, wo_gdn, g, b):
    M, D = x.shape
    tm = min(ROW_TILE, M)

    def rows(width):
        return pl.BlockSpec((tm, width), lambda i: (i, 0))

    def full(a):
        return pl.BlockSpec(a.shape, lambda i: (0, 0))

    return pl.pallas_call(
        _mixer_body,
        grid=(M // tm,),
        in_specs=[rows(D), rows(ATT_WIDTH), rows(GDN_WIDTH), full(wo_att), full(wo_gdn), full(g), full(b)],
        out_specs=rows(D),
        out_shape=jax.ShapeDtypeStruct((M, D), F32),
        compiler_params=_cparams(("parallel",)),
        name="mixer_residual",
    )(x, att, gdn, wo_att, wo_gdn, g, b)


def _rope_tables(pos):
    d_rot = HEAD_DIM // 4
    half = d_rot // 2
    inv_freq = ROPE_THETA ** (-jnp.arange(half, dtype=F32) / half)
    ang = pos.astype(F32)[:, None] * inv_freq[None, :]
    cos = jnp.cos(ang)
    sin = jnp.sin(ang)
    n = pos.shape[0]
    ones = jnp.ones((n, HEAD_DIM - d_rot), F32)
    zeros = jnp.zeros((n, HEAD_DIM - d_rot), F32)
    zh = jnp.zeros((n, half), F32)
    c = jnp.concatenate([cos, cos, ones], axis=1)
    sa = jnp.concatenate([-sin, zh, zeros], axis=1)
    sb = jnp.concatenate([zh, sin, zeros], axis=1)
    rep = LANES // HEAD_DIM
    return jnp.tile(c, (1, rep)), jnp.tile(sa, (1, rep)), jnp.tile(sb, (1, rep))


def _split_w_in(w_in):
    pts = [0]
    for s in SPLITS:
        pts.append(pts[-1] + s)
    wq, wk, wv, wqi, wki, wwi, wqkv, wz, wb, wa = [w_in[:, pts[n]:pts[n + 1]] for n in range(len(SPLITS))]
    D = w_in.shape[0]
    wr = jnp.concatenate([wq, wqi, wk, wki, jnp.zeros((D, ROPE_W - 2 * ATT_WIDTH - LANES - IDX_DIM), F32)], axis=1)
    ws = jnp.concatenate([wwi, wb, wa, jnp.zeros((D, LANES - IDX_HEADS - 2 * GDN_HEADS), F32)], axis=1)
    return tuple(a.astype(BF16) for a in (wr, wv, ws, wqkv, wz))


def kernel(x_prompt, x_sample, cache_k, cache_v, cache_kidx, state_gdn, state_conv, page_table, ffn1_w_gate, ffn1_w_up, ffn1_w_down, ln1_g, ln1_b, w_in, conv_w, a_log, dt_bias, gdn_norm_w, w_o, ln2_g, ln2_b, ffn2_w_gate, ffn2_w_up, ffn2_w_down, ln3_g, ln3_b):
    B, S, D = x_prompt.shape
    Bs, T, _ = x_sample.shape
    n_pages = page_table.shape[1]
    page = cache_k.shape[2]
    past = n_pages * page
    kvw = N_KV_HEADS * HEAD_DIM
    l = 0
    xp = x_prompt.reshape(B * S, D)
    xs = x_sample.reshape(Bs * T, D)

    f1 = (ffn1_w_gate[l].astype(BF16), ffn1_w_up[l].astype(BF16), ffn1_w_down[l].astype(BF16), ln1_g, ln1_b)
    f2 = (ffn2_w_gate[l].astype(BF16), ffn2_w_up[l].astype(BF16), ffn2_w_down[l].astype(BF16), ln3_g, ln3_b)
    w_proj = _split_w_in(w_in[l])
    wo_att = w_o[l][:ATT_WIDTH].astype(BF16)
    wo_gdn = w_o[l][ATT_WIDTH:].astype(BF16)

    xp1 = _ffn_half_step(xp, *f1)
    tm_p = min(ROW_TILE, B * S)
    q, qi, kt, kit, vt, small, qkv, z = _mix_projection(xp1, w_proj, _rope_tables(jnp.arange(S)), S // tm_p, seq=S)
    att = _dsa_prompt(q, qi, small, kt, kit, vt)
    gdn, s_p, c_p = _gdn(qkv, z, small, _gates_transposed(small, B, S), conv_w[l], a_log[l], dt_bias[l],
                         gdn_norm_w[l], jnp.zeros((B, CONV_K - 1, GDN_CONV_DIM), F32),
                         jnp.zeros((B, GDN_HEADS, GDN_DK, GDN_DV), F32), B, S)
    xp2 = _mixer_residual(xp1, att, gdn, wo_att, wo_gdn, ln2_g, ln2_b)
    y_prompt = _ffn_half_step(xp2, *f2).reshape(B, S, D)
    k_prompt = kt.reshape(1, B, N_KV_HEADS, HEAD_DIM, S).transpose(0, 1, 4, 2, 3)
    v_prompt = vt.reshape(1, B, N_KV_HEADS, HEAD_DIM, S).transpose(0, 1, 4, 2, 3)
    kidx_prompt = kit.reshape(1, B, IDX_DIM, S).transpose(0, 1, 3, 2)

    xs1 = _ffn_half_step(xs, *f1)
    pos_s = jnp.tile(past + jnp.arange(T), Bs)
    tm_s = min(ROW_TILE, Bs * T)
    q, qi, k, ki, v, small, qkv, z = _mix_projection(xs1, w_proj, _rope_tables(pos_s), Bs * T // tm_s)
    group = ATT_HEADS // N_KV_HEADS
    qh = q.reshape(Bs, T, N_KV_HEADS, group, HEAD_DIM).transpose(0, 2, 3, 1, 4)
    zq = jnp.zeros_like(qh[:, 0])
    q_ht = jnp.concatenate([jnp.concatenate([qh[:, 0], zq], -1), jnp.concatenate([zq, qh[:, 1]], -1)], axis=1)
    q_ht = q_ht.reshape(Bs, ATT_HEADS * T, LANES)
    qi_ht = qi.reshape(Bs, T, IDX_HEADS, IDX_DIM).transpose(0, 2, 1, 3).reshape(Bs, IDX_HEADS * T, IDX_DIM)

    def new_tile(a):
        a = a.reshape(Bs, T, a.shape[-1]).transpose(0, 2, 1)
        return jnp.pad(a, ((0, 0), (0, 0), (0, LANES - T)))

    cki_t = cache_kidx[l].transpose(0, 2, 1)
    ck_t = cache_k[l].transpose(0, 2, 3, 1).reshape(-1, kvw, page)
    cv_t = cache_v[l].transpose(0, 2, 3, 1).reshape(-1, kvw, page)
    o_ht = _dsa_sample(page_table, qi_ht, small.reshape(Bs, T, LANES), q_ht, new_tile(ki), new_tile(k), new_tile(v),
                       cki_t, ck_t, cv_t)
    o_ht = o_ht.reshape(Bs, N_KV_HEADS, group, T, N_KV_HEADS, HEAD_DIM)
    att = jnp.stack([o_ht[:, 0, :, :, 0], o_ht[:, 1, :, :, 1]], axis=1)
    att = att.transpose(0, 3, 1, 2, 4).reshape(Bs * T, ATT_WIDTH)
    gdn, s_s, c_s = _gdn(qkv, z, small, _gates_transposed(small, Bs, T), conv_w[l], a_log[l], dt_bias[l],
                         gdn_norm_w[l], state_conv[l], state_gdn[l], Bs, T)
    xs2 = _mixer_residual(xs1, att, gdn, wo_att, wo_gdn, ln2_g, ln2_b)
    y_sample = _ffn_half_step(xs2, *f2).reshape(Bs, T, D)

    return (y_prompt, y_sample,
            k_prompt, v_prompt, kidx_prompt, s_p[None], c_p[None],
            k.reshape(1, Bs, T, N_KV_HEADS, HEAD_DIM), v.reshape(1, Bs, T, N_KV_HEADS, HEAD_DIM),
            ki.reshape(1, Bs, T, IDX_DIM), s_s[None], c_s[None])
```

```python
import functools
import math

import jax
import jax.numpy as jnp
from jax import lax
from jax.experimental import pallas as pl
from jax.experimental.pallas import tpu as pltpu

F32 = jnp.float32
BF16 = jnp.bfloat16
HI = lax.Precision.HIGHEST

LANES = 128
VMEM_LIMIT = 56 * 1024 * 1024

D_MODEL = 1024
DEPTH = 1
ATT_HEADS = 8
N_KV_HEADS = 2
HEAD_DIM = 64
ATT_WIDTH = ATT_HEADS * HEAD_DIM
ROPE_THETA = 500000.0
IDX_HEADS = 8
IDX_DIM = 64
TOPK_MAX = 256
GDN_HEADS = 4
GDN_DK = 128
GDN_DV = 128
GDN_WIDTH = GDN_HEADS * GDN_DV
GDN_CONV_DIM = 2 * GDN_HEADS * GDN_DK + GDN_WIDTH
CONV_K = 4
GDN_CHUNK = 64
LN_EPS = 1e-5
NORM_EPS = 1e-6
DEEP_ALPHA = (2 * DEPTH) ** 0.25
SPLITS = (ATT_WIDTH, N_KV_HEADS * HEAD_DIM, N_KV_HEADS * HEAD_DIM, IDX_HEADS * IDX_DIM, IDX_DIM, IDX_HEADS,
          GDN_CONV_DIM, GDN_WIDTH, GDN_HEADS, GDN_HEADS)
ROPE_W = 1280
TOPK_SEARCH_PROMPT = (20, 2)
TOPK_SEARCH_DECODE = (10, 4)

NEG_INF = float("-inf")
POS_INF = float("inf")
NN = (((1,), (0,)), ((), ()))
NT = (((1,), (1,)), ((), ()))
TN = (((0,), (0,)), ((), ()))
ROW_TILE = 512
QUERY_TILE = 256
TOPK_ROWS = 128
IDX_KEY_CHUNK = 128
CAUSAL_CLASSES = 4
GDN_MM_GRAM = "bf16"
GDN_MM_SOLVE = "x3"
GDN_SOLVE_FINE_LEVELS = 2
GDN_CHUNKS_PER_STEP = 2
GDN_MM_STATE = "bf16"


def _cparams(sem):
    return pltpu.CompilerParams(dimension_semantics=sem, vmem_limit_bytes=VMEM_LIMIT)


def _layer_norm_rows(y, g, b):
    mu = jnp.mean(y, axis=-1, keepdims=True)
    d = y - mu
    var = jnp.mean(d * d, axis=-1, keepdims=True)
    return d * lax.rsqrt(var + LN_EPS) * g + b


def _silu(x):
    return x * jax.nn.sigmoid(x)


def _softplus(x):
    return jnp.maximum(x, 0.0) + jnp.log(1.0 + jnp.exp(-jnp.abs(x)))


def _split_bf16(x):
    hi = x.astype(BF16)
    return hi, (x - hi.astype(F32)).astype(BF16)


def _mm(a, b, dims, mode):
    if mode == "f32":
        return lax.dot_general(a, b, dims, precision=HI, preferred_element_type=F32)
    dot = functools.partial(lax.dot_general, dimension_numbers=dims, preferred_element_type=F32)
    if mode == "bf16":
        return dot(a.astype(BF16), b.astype(BF16))
    ah, al = _split_bf16(a)
    bh, bl = _split_bf16(b)
    return dot(ah, bh) + (dot(ah, bl) + dot(al, bh))


def _ffn_body(x_ref, wg_ref, wu_ref, wd_ref, g_ref, b_ref, o_ref, acc_ref):
    j = pl.program_id(1)
    xb = x_ref[...].astype(BF16)
    hg = jnp.dot(xb, wg_ref[...], preferred_element_type=F32)
    hu = jnp.dot(xb, wu_ref[...], preferred_element_type=F32)
    h = _silu(hg) * hu
    part = jnp.dot(h.astype(BF16), wd_ref[...], preferred_element_type=F32)

    @pl.when(j == 0)
    def _():
        acc_ref[...] = part

    @pl.when(j > 0)
    def _():
        acc_ref[...] += part

    @pl.when(j == pl.num_programs(1) - 1)
    def _():
        y = DEEP_ALPHA * x_ref[...] + 0.5 * acc_ref[...]
        o_ref[...] = _layer_norm_rows(y, g_ref[...], b_ref[...])


def _ffn_half_step(x, wg, wu, wd, g, b):
    M, D = x.shape
    FF = wg.shape[1]
    tm = min(ROW_TILE, M)
    tf = FF // 2 if (FF // 2) % LANES == 0 else FF
    return pl.pallas_call(
        _ffn_body,
        grid=(M // tm, FF // tf),
        in_specs=[
            pl.BlockSpec((tm, D), lambda i, j: (i, 0)),
            pl.BlockSpec((D, tf), lambda i, j: (0, j)),
            pl.BlockSpec((D, tf), lambda i, j: (0, j)),
            pl.BlockSpec((tf, D), lambda i, j: (j, 0)),
            pl.BlockSpec((1, D), lambda i, j: (0, 0)),
            pl.BlockSpec((1, D), lambda i, j: (0, 0)),
        ],
        out_specs=pl.BlockSpec((tm, D), lambda i, j: (i, 0)),
        out_shape=jax.ShapeDtypeStruct((M, D), F32),
        scratch_shapes=[pltpu.VMEM((tm, D), F32)],
        compiler_params=_cparams(("parallel", "arbitrary")),
        name="ffn_half_step",
    )(x, wg, wu, wd, g, b)


def _proj_body(x_ref, wr_ref, wv_ref, ws_ref, wqkv_ref, wz_ref, cos_ref, sa_ref, sb_ref,
               q_ref, qi_ref, k_ref, ki_ref, v_ref, small_ref, qkv_ref, z_ref, *, transposed):
    xb = x_ref[...].astype(BF16)
    r = jnp.dot(xb, wr_ref[...], preferred_element_type=F32)
    c = cos_ref[...]
    sa = sa_ref[...]
    sb = sb_ref[...]

    def rope(slab):
        return slab * c + pltpu.roll(slab, LANES - 8, 1) * sa + pltpu.roll(slab, 8, 1) * sb

    for s in range(ATT_WIDTH // LANES):
        q_ref[:, s * LANES:(s + 1) * LANES] = rope(r[:, s * LANES:(s + 1) * LANES])
    off = ATT_WIDTH
    for s in range(IDX_HEADS * IDX_DIM // LANES):
        qi_ref[:, s * LANES:(s + 1) * LANES] = rope(r[:, off + s * LANES:off + (s + 1) * LANES])
    off += IDX_HEADS * IDX_DIM
    kr = rope(r[:, off:off + LANES])
    kir = rope(r[:, off + LANES:off + 2 * LANES])
    v = jnp.dot(xb, wv_ref[...], preferred_element_type=F32)
    if transposed:
        k_ref[0] = kr.T
        ki_ref[0] = kir.T[:IDX_DIM, :]
        v_ref[0] = v.T
    else:
        k_ref[...] = kr
        ki_ref[...] = kir[:, :IDX_DIM]
        v_ref[...] = v
    sm = jnp.dot(xb, ws_ref[...], preferred_element_type=F32)
    lane = lax.broadcasted_iota(jnp.int32, sm.shape, 1)
    small_ref[...] = jnp.where(lane < IDX_HEADS, sm * (IDX_HEADS ** -0.5), sm)
    qkv_ref[...] = jnp.dot(xb, wqkv_ref[...], preferred_element_type=F32)
    z_ref[...] = jnp.dot(xb, wz_ref[...], preferred_element_type=F32)


def _mix_projection(x, w, tabs, n_tab_blocks, seq=None):
    M, D = x.shape
    tm = min(ROW_TILE, M)
    wr, wv, ws, wqkv, wz = w
    cos_t, sa_t, sb_t = tabs
    kvw = N_KV_HEADS * HEAD_DIM

    def full(a):
        return pl.BlockSpec(a.shape, lambda i: (0, 0))

    def rows(width):
        return pl.BlockSpec((tm, width), lambda i: (i, 0))

    tab_spec = pl.BlockSpec((tm, LANES), lambda i: (i % n_tab_blocks, 0))
    if seq is None:
        kv_shapes = [jax.ShapeDtypeStruct((M, kvw), F32), jax.ShapeDtypeStruct((M, IDX_DIM), F32),
                     jax.ShapeDtypeStruct((M, kvw), F32)]
        kv_specs = [rows(kvw), rows(IDX_DIM), rows(kvw)]
    else:
        nt = seq // tm
        nb = M // seq

        def tr(width):
            return pl.BlockSpec((1, width, tm), lambda i: (i // nt, 0, i % nt))

        kv_shapes = [jax.ShapeDtypeStruct((nb, kvw, seq), F32), jax.ShapeDtypeStruct((nb, IDX_DIM, seq), F32),
                     jax.ShapeDtypeStruct((nb, kvw, seq), F32)]
        kv_specs = [tr(kvw), tr(IDX_DIM), tr(kvw)]
    out_shapes = [
        jax.ShapeDtypeStruct((M, ATT_WIDTH), F32),
        jax.ShapeDtypeStruct((M, IDX_HEADS * IDX_DIM), F32),
    ] + kv_shapes + [
        jax.ShapeDtypeStruct((M, LANES), F32),
        jax.ShapeDtypeStruct((M, GDN_CONV_DIM), F32),
        jax.ShapeDtypeStruct((M, GDN_WIDTH), F32),
    ]
    out_specs = [rows(ATT_WIDTH), rows(IDX_HEADS * IDX_DIM)] + kv_specs + [
        rows(LANES), rows(GDN_CONV_DIM), rows(GDN_WIDTH)]
    return pl.pallas_call(
        functools.partial(_proj_body, transposed=seq is not None),
        grid=(M // tm,),
        in_specs=[rows(D), full(wr), full(wv), full(ws), full(wqkv), full(wz), tab_spec, tab_spec, tab_spec],
        out_specs=out_specs,
        out_shape=out_shapes,
        compiler_params=_cparams(("parallel",)),
        name="mix_projection",
    )(x, wr, wv, ws, wqkv, wz, cos_t, sa_t, sb_t)


def _count_ge(xm, thr):
    return jnp.sum(jnp.where(xm >= thr, 1.0, 0.0), axis=-1, keepdims=True)


def _prefix_count(eq_f32):
    R, L = eq_f32.shape
    ri = lax.broadcasted_iota(jnp.int32, (LANES, LANES), 0)
    ci = lax.broadcasted_iota(jnp.int32, (LANES, LANES), 1)
    tri = jnp.where(ri <= ci, 1.0, 0.0).astype(BF16)
    offset = jnp.zeros((R, 1), F32)
    pieces = []
    for blk in range(L // LANES):
        e = eq_f32[:, blk * LANES:(blk + 1) * LANES].astype(BF16)
        loc = jnp.dot(e, tri, preferred_element_type=F32)
        pieces.append(loc + offset)
        offset = offset + loc[:, LANES - 1:LANES]
    return jnp.concatenate(pieces, axis=1)


def _topk_bias(xms, k, n_steps, arity):
    kf = float(k)
    n = len(xms)
    grp = range(n)
    rowmax = [jnp.max(xm, axis=-1, keepdims=True) for xm in xms]
    lo0 = [jnp.min(jnp.where(xm == NEG_INF, POS_INF, xm), axis=-1, keepdims=True) for xm in xms]
    hi0 = [m + jnp.abs(m) * (2.0 ** -10) + 1.0 for m in rowmax]

    def search(_, c):
        lo, hi = c[:n], c[n:]
        mids = [[lo[g] + (hi[g] - lo[g]) * (j / arity) for j in range(1, arity)] for g in grp]
        ge = [[_count_ge(xms[g], m) >= kf for m in mids[g]] for g in grp]
        new_lo, new_hi = [], []
        for g in grp:
            nlo, nhi = lo[g], hi[g]
            for j in range(arity - 1):
                nlo = jnp.where(ge[g][j], mids[g][j], nlo)
            for j in reversed(range(arity - 1)):
                nhi = jnp.where(ge[g][j], nhi, mids[g][j])
            new_lo.append(nlo)
            new_hi.append(nhi)
        return tuple(new_lo) + tuple(new_hi)

    c = lax.fori_loop(0, n_steps, search, tuple(lo0) + tuple(hi0))
    lo, hi = list(c[:n]), list(c[n:])
    c_lo = [_count_ge(xms[g], lo[g]) for g in grp]
    pending = [jnp.where(c_lo[g] > kf, 1.0, 0.0) for g in grp]

    def cond(c):
        worst = jnp.max(c[3 * n])
        for g in range(1, n):
            worst = jnp.maximum(worst, jnp.max(c[3 * n + g]))
        return worst > 0.0

    def body(c):
        lo, hi, c_lo, pending = c[:n], c[n:2 * n], c[2 * n:3 * n], c[3 * n:]
        t = [jnp.max(jnp.where(xms[g] < hi[g], xms[g], NEG_INF), axis=-1, keepdims=True) for g in grp]
        ct = [_count_ge(xms[g], t[g]) for g in grp]
        out = [[], [], [], []]
        for g in grp:
            hit = ct[g] >= kf
            live = pending[g] > 0.0
            upd = jnp.logical_and(live, hit)
            out[0].append(jnp.where(upd, t[g], lo[g]))
            out[1].append(jnp.where(jnp.logical_and(live, jnp.logical_not(hit)), t[g], hi[g]))
            out[2].append(jnp.where(upd, ct[g], c_lo[g]))
            out[3].append(jnp.where(hit, 0.0, pending[g]))
        return tuple(out[0]) + tuple(out[1]) + tuple(out[2]) + tuple(out[3])

    c = lax.while_loop(cond, body, tuple(lo) + tuple(hi) + tuple(c_lo) + tuple(pending))
    lo, c_lo = c[:n], c[2 * n:3 * n]

    def select(xm, lo, c_lo):
        def plain():
            return jnp.where(xm >= lo, 0.0, NEG_INF)

        def with_ties():
            gt = xm > lo
            eq = xm == lo
            neg_zero = jnp.logical_and(xm == 0.0, 1.0 / xm < 0.0)
            eq_hi = jnp.logical_and(eq, jnp.logical_not(neg_zero))
            eq_lo = jnp.logical_and(eq, neg_zero)
            room = kf - jnp.sum(jnp.where(gt, 1.0, 0.0), axis=-1, keepdims=True)
            rank_hi = _prefix_count(jnp.where(eq_hi, 1.0, 0.0))
            rank_lo = _prefix_count(jnp.where(eq_lo, 1.0, 0.0)) + rank_hi[:, -1:]
            keep = jnp.logical_or(gt, jnp.logical_or(jnp.logical_and(eq_hi, rank_hi <= room),
                                                     jnp.logical_and(eq_lo, rank_lo <= room)))
            return jnp.where(keep, 0.0, NEG_INF)

        return lax.cond(jnp.max(c_lo) > kf, with_ties, plain)

    return [select(xms[g], lo[g], c_lo[g]) for g in grp]


def _indexer_scores(qi_bf, kit_ref, wi, sk):
    ws = wi * (IDX_DIM ** -0.5)
    pieces = []
    for kc in range(sk // IDX_KEY_CHUNK):
        kit = kit_ref[0, :, kc * IDX_KEY_CHUNK:(kc + 1) * IDX_KEY_CHUNK].astype(BF16)
        acc = None
        for h in range(IDX_HEADS):
            s = jnp.dot(qi_bf[:, h * IDX_DIM:(h + 1) * IDX_DIM], kit, preferred_element_type=F32)
            term = jnp.maximum(s, 0.0) * ws[:, h:h + 1]
            acc = term if acc is None else acc + term
        pieces.append(acc)
    return jnp.concatenate(pieces, axis=1)


def _dsa_prompt_block(q_ref, qi_ref, sm_ref, kt_ref, kit_ref, vt_ref, o_ref, i, *, tq, ksel, sk):
    q_pos = i * tq + lax.broadcasted_iota(jnp.int32, (tq, sk), 0)
    k_pos = lax.broadcasted_iota(jnp.int32, (tq, sk), 1)
    score = _indexer_scores(qi_ref[...].astype(BF16), kit_ref, sm_ref[...], sk)
    xm = jnp.where(k_pos <= q_pos, score, NEG_INF)
    groups = [xm[r:r + TOPK_ROWS] for r in range(0, tq, TOPK_ROWS)]
    bias = jnp.concatenate(_topk_bias(groups, ksel, *TOPK_SEARCH_PROMPT), axis=0)

    kt = kt_ref[0, :, :sk].astype(BF16)
    vt = vt_ref[0, :, :sk].astype(BF16)
    qb = (q_ref[...] * (HEAD_DIM ** -0.5)).astype(BF16)
    group = ATT_HEADS // N_KV_HEADS
    lane = lax.broadcasted_iota(jnp.int32, (tq, LANES), 1)
    outs = []
    for h in range(ATT_HEADS):
        kv = h // group
        s = jnp.dot(qb[:, h * HEAD_DIM:(h + 1) * HEAD_DIM], kt[kv * HEAD_DIM:(kv + 1) * HEAD_DIM, :],
                    preferred_element_type=F32) + bias
        m = jnp.max(s, axis=-1, keepdims=True)
        p = jnp.exp(s - m)
        l = jnp.sum(p, axis=-1, keepdims=True)
        outs.append(lax.dot_general(p.astype(BF16), vt, NT, preferred_element_type=F32) / l)
    for pair in range(ATT_HEADS // 2):
        h0, h1 = 2 * pair, 2 * pair + 1
        kv = h0 // group
        a, b = outs[h0], outs[h1]
        if kv == 0:
            b = pltpu.roll(b, HEAD_DIM, 1)
        else:
            a = pltpu.roll(a, HEAD_DIM, 1)
        o_ref[:, pair * LANES:(pair + 1) * LANES] = jnp.where(lane < HEAD_DIM, a, b)


def _dsa_prompt_body(q_ref, qi_ref, sm_ref, kt_ref, kit_ref, vt_ref, o_ref, *, tq, ksel, n_classes):
    i = pl.program_id(1)
    per = kt_ref.shape[2] // tq // n_classes
    for c in range(n_classes):
        @pl.when(jnp.logical_and(i >= c * per, i < (c + 1) * per))
        def _(c=c):
            _dsa_prompt_block(q_ref, qi_ref, sm_ref, kt_ref, kit_ref, vt_ref, o_ref, i,
                              tq=tq, ksel=ksel, sk=(c + 1) * per * tq)


def _dsa_prompt(q, qi, small, kt, kit, vt):
    B, kvw, S = kt.shape
    tq = QUERY_TILE
    nq = S // tq
    ksel = min(TOPK_MAX, S // 4)
    n_classes = CAUSAL_CLASSES if nq % CAUSAL_CLASSES == 0 and (nq // CAUSAL_CLASSES * tq) % IDX_KEY_CHUNK == 0 else 1
    return pl.pallas_call(
        functools.partial(_dsa_prompt_body, tq=tq, ksel=ksel, n_classes=n_classes),
        grid=(B, nq),
        in_specs=[
            pl.BlockSpec((tq, ATT_WIDTH), lambda b, i: (b * nq + i, 0)),
            pl.BlockSpec((tq, IDX_HEADS * IDX_DIM), lambda b, i: (b * nq + i, 0)),
            pl.BlockSpec((tq, LANES), lambda b, i: (b * nq + i, 0)),
            pl.BlockSpec((1, kvw, S), lambda b, i: (b, 0, 0)),
            pl.BlockSpec((1, IDX_DIM, S), lambda b, i: (b, 0, 0)),
            pl.BlockSpec((1, kvw, S), lambda b, i: (b, 0, 0)),
        ],
        out_specs=pl.BlockSpec((tq, ATT_WIDTH), lambda b, i: (b * nq + i, 0)),
        out_shape=jax.ShapeDtypeStruct((B * S, ATT_WIDTH), F32),
        compiler_params=_cparams(("parallel", "arbitrary")),
        name="dsa_prompt",
    )(q, qi, small, kt, kit, vt)


def _dsa_sample_body(pt_ref, qi_ref, wi_ref, q_ref, kin_ref, kn_ref, vn_ref, cki_hbm, ck_hbm, cv_hbm, o_ref,
                     kibuf, kbuf, vbuf, sem, *, n_pages, page, t_new, ksel):
    b = pl.program_id(0)
    nb = pl.num_programs(0)
    slot = lax.rem(b, 2)
    past = n_pages * page
    lp = kibuf.shape[2]

    def page_copies(bb, sl, p):
        pg = pt_ref[bb, p]
        cols = pl.ds(p * page, page)
        return (pltpu.make_async_copy(cki_hbm.at[pg], kibuf.at[sl, :, cols], sem.at[sl, 0]),
                pltpu.make_async_copy(ck_hbm.at[pg], kbuf.at[sl, :, cols], sem.at[sl, 1]),
                pltpu.make_async_copy(cv_hbm.at[pg], vbuf.at[sl, :, cols], sem.at[sl, 2]))

    def start_all(bb, sl):
        for p in range(n_pages):
            for cp in page_copies(bb, sl, p):
                cp.start()

    def wait_all(bb, sl):
        for p in range(n_pages):
            for cp in page_copies(bb, sl, p):
                cp.wait()

    @pl.when(b == 0)
    def _():
        start_all(0, 0)

    @pl.when(b + 1 < nb)
    def _():
        start_all(b + 1, 1 - slot)

    kibuf[slot, :, past:lp] = kin_ref[0]
    kbuf[slot, :, past:lp] = kn_ref[0]
    vbuf[slot, :, past:lp] = vn_ref[0]

    wait_all(b, slot)

    s_idx = jnp.dot(qi_ref[0].astype(BF16), kibuf[slot].astype(BF16),
                    preferred_element_type=F32)
    wi = wi_ref[0] * (IDX_DIM ** -0.5)
    score = None
    for h in range(IDX_HEADS):
        term = jnp.maximum(s_idx[h * t_new:(h + 1) * t_new, :], 0.0) * wi[:, h:h + 1]
        score = term if score is None else score + term
    q_pos = past + lax.broadcasted_iota(jnp.int32, (t_new, lp), 0)
    k_pos = lax.broadcasted_iota(jnp.int32, (t_new, lp), 1)
    bias = _topk_bias([jnp.where(k_pos <= q_pos, score, NEG_INF)], ksel, *TOPK_SEARCH_DECODE)[0]

    qb = (q_ref[0] * (HEAD_DIM ** -0.5)).astype(BF16)
    s = jnp.dot(qb, kbuf[slot].astype(BF16), preferred_element_type=F32)
    s = s + jnp.concatenate([bias] * ATT_HEADS, axis=0)
    m = jnp.max(s, axis=-1, keepdims=True)
    p = jnp.exp(s - m)
    l = jnp.sum(p, axis=-1, keepdims=True)
    o_ref[0] = lax.dot_general(p.astype(BF16), vbuf[slot].astype(BF16), NT, preferred_element_type=F32) / l


def _dsa_sample(page_table, qi_ht, wi, q_ht, kit_new, kt_new, vt_new, cki_t, ck_t, cv_t):
    B, n_pages = page_table.shape
    page = cki_t.shape[2]
    t_new = wi.shape[1]
    past = n_pages * page
    lp = past + LANES
    ksel = min(TOPK_MAX, (past + t_new) // 4)
    rows = ATT_HEADS * t_new
    kvw = N_KV_HEADS * HEAD_DIM
    grid_spec = pltpu.PrefetchScalarGridSpec(
        num_scalar_prefetch=1,
        grid=(B,),
        in_specs=[
            pl.BlockSpec((1, rows, IDX_DIM), lambda b, pt: (b, 0, 0)),
            pl.BlockSpec((1, t_new, LANES), lambda b, pt: (b, 0, 0)),
            pl.BlockSpec((1, rows, LANES), lambda b, pt: (b, 0, 0)),
            pl.BlockSpec((1, IDX_DIM, LANES), lambda b, pt: (b, 0, 0)),
            pl.BlockSpec((1, kvw, LANES), lambda b, pt: (b, 0, 0)),
            pl.BlockSpec((1, kvw, LANES), lambda b, pt: (b, 0, 0)),
            pl.BlockSpec(memory_space=pl.ANY),
            pl.BlockSpec(memory_space=pl.ANY),
            pl.BlockSpec(memory_space=pl.ANY),
        ],
        out_specs=pl.BlockSpec((1, rows, LANES), lambda b, pt: (b, 0, 0)),
        scratch_shapes=[
            pltpu.VMEM((2, IDX_DIM, lp), F32),
            pltpu.VMEM((2, kvw, lp), F32),
            pltpu.VMEM((2, kvw, lp), F32),
            pltpu.SemaphoreType.DMA((2, 3)),
        ],
    )
    return pl.pallas_call(
        functools.partial(_dsa_sample_body, n_pages=n_pages, page=page, t_new=t_new, ksel=ksel),
        grid_spec=grid_spec,
        out_shape=jax.ShapeDtypeStruct((B, rows, LANES), F32),
        compiler_params=_cparams(("arbitrary",)),
        name="dsa_sample",
    )(page_table, qi_ht, wi, q_ht, kit_new, kt_new, vt_new, cki_t, ck_t, cv_t)


def _gdn_body(qkv_ref, z_ref, sm_ref, gt_ref, cw_ref, alr_ref, dtr_ref, alc_ref, dtc_ref, nw_ref, c0_ref, s0_ref,
              o_ref, sT_ref, cT_ref, xbuf, qs, ks, vs, bcol, gcol, gcc, gcr, us, ws, qes, kds, qks,
              s0_scr, s1_scr, s2_scr, s3_scr, *, ti, tt, chunk):
    i = pl.program_id(1)
    n_i = pl.num_programs(1)
    nq = GDN_HEADS * GDN_DK
    ncb = tt // chunk
    halo = 8
    s_scr = (s0_scr, s1_scr, s2_scr, s3_scr)

    @pl.when(i == 0)
    def _():
        for h in range(GDN_HEADS):
            s_scr[h][...] = s0_ref[0, h]
        xbuf[halo - (CONV_K - 1):halo, :] = c0_ref[0]

    xbuf[halo:halo + ti, :] = qkv_ref[...]
    new_tail = xbuf[halo + ti - (CONV_K - 1):halo + ti, :]
    cw = cw_ref[...]
    y = None
    for j in range(CONV_K):
        term = xbuf[halo - (CONV_K - 1) + j:halo - (CONV_K - 1) + j + ti, :] * cw[j:j + 1, :]
        y = term if y is None else y + term
    y = _silu(y)
    xbuf[halo - (CONV_K - 1):halo, :] = new_tail

    @pl.when(i == n_i - 1)
    def _():
        cT_ref[0] = new_tail

    if ti < tt:
        zpad = jnp.zeros((tt - ti, nq), F32)
        qs[ti:tt, :] = zpad
        ks[ti:tt, :] = zpad
        vs[ti:tt, :] = zpad
        bcol[ti:tt, :] = zpad[:, :LANES]
        gcol[ti:tt, :] = zpad[:, :LANES]
    for h in range(GDN_HEADS):
        qh = y[:, h * GDN_DK:(h + 1) * GDN_DK]
        kh = y[:, nq + h * GDN_DK:nq + (h + 1) * GDN_DK]
        qs[0:ti, h * GDN_DK:(h + 1) * GDN_DK] = (qh * lax.rsqrt(jnp.sum(qh * qh, -1, keepdims=True) + NORM_EPS)
                                                 * (GDN_DK ** -0.5))
        ks[0:ti, h * GDN_DK:(h + 1) * GDN_DK] = kh * lax.rsqrt(jnp.sum(kh * kh, -1, keepdims=True) + NORM_EPS)
    vs[0:ti, :] = y[:, 2 * nq:]
    sm = sm_ref[...]
    bcol[0:ti, :] = jax.nn.sigmoid(sm)
    gcol[0:ti, :] = -jnp.exp(alr_ref[...]) * _softplus(sm + dtr_ref[...])

    ri = lax.broadcasted_iota(jnp.int32, (chunk, chunk), 0)
    ci = lax.broadcasted_iota(jnp.int32, (chunk, chunk), 1)
    causal = ri >= ci
    strict = ri > ci
    ltri = jnp.where(causal, 1.0, 0.0)
    utri = jnp.where(ri <= ci, 1.0, 0.0)
    col_live = lax.broadcasted_iota(jnp.int32, (8, chunk), 1) < ti
    n_levels = int(math.log2(chunk))

    for c in range(ncb):
        gcc[c * chunk:(c + 1) * chunk, :] = _mm(ltri, gcol[c * chunk:(c + 1) * chunk, :], NN, "f32")
        g_rows = -jnp.exp(alc_ref[...]) * _softplus(gt_ref[c] + dtc_ref[...])
        gcr[c] = _mm(jnp.where(col_live, g_rows, 0.0), utri, NN, "f32")

    cpi = GDN_CHUNKS_PER_STEP if ncb % GDN_CHUNKS_PER_STEP == 0 else 1

    def factor_step(c, carry):
        prob = range(cpi * GDN_HEADS)
        hd = [p % GDN_HEADS for p in prob]
        rows = [pl.ds(pl.multiple_of((c * cpi + p // GDN_HEADS) * chunk, chunk), chunk) for p in prob]
        hsl = [slice(h * GDN_DK, (h + 1) * GDN_DK) for h in hd]
        gcum_col = [gcc[rows[j * GDN_HEADS], :] for j in range(cpi)]
        gcum_row = [gcr[c * cpi + j] for j in range(cpi)]
        b_all = [bcol[rows[j * GDN_HEADS], :] for j in range(cpi)]
        kc = [ks[rows[p], hsl[p]] for p in prob]
        qc = [qs[rows[p], hsl[p]] for p in prob]
        beta = [b_all[p // GDN_HEADS][:, 8 + hd[p]:9 + hd[p]] for p in prob]
        gc = [gcum_col[p // GDN_HEADS][:, 12 + hd[p]:13 + hd[p]] for p in prob]
        eg = [jnp.exp(gc[p]) for p in prob]
        decay = [jnp.exp(jnp.where(causal, gc[p] - gcum_row[p // GDN_HEADS][4 + hd[p]:5 + hd[p], :], NEG_INF))
                 for p in prob]
        kb = [kc[p] * beta[p] for p in prob]
        gram = [_mm(jnp.concatenate([kb[p], qc[p]], axis=0), kc[p], NT, GDN_MM_GRAM) for p in prob]
        a = [-jnp.where(strict, gram[p][:chunk] * decay[p], 0.0) for p in prob]
        sol = [jnp.concatenate([vs[rows[p], hsl[p]] * beta[p], kb[p] * eg[p]], axis=1) for p in prob]
        for lvl in range(n_levels):
            mode = GDN_MM_SOLVE if lvl < GDN_SOLVE_FINE_LEVELS else "bf16"
            sol = [sol[p] + _mm(a[p], sol[p], NN, mode) for p in prob]
            if lvl + 1 < n_levels:
                a = [_mm(a[p], a[p], NN, "bf16") for p in prob]
        for p in prob:
            qks[hd[p], rows[p], :] = jnp.where(causal, gram[p][chunk:] * decay[p], 0.0)
            us[rows[p], hsl[p]] = sol[p][:, :GDN_DV]
            ws[rows[p], hsl[p]] = sol[p][:, GDN_DV:]
            qes[rows[p], hsl[p]] = qc[p] * eg[p]
            kds[rows[p], hsl[p]] = kc[p] * jnp.exp(gc[p][chunk - 1:chunk, :] - gc[p])
        return carry

    lax.fori_loop(0, ncb // cpi, factor_step, 0)

    def state_step(c, carry):
        r0 = pl.multiple_of(c * chunk, chunk)
        rows = pl.ds(r0, chunk)
        g_last_all = gcc[pl.ds(r0 + chunk - 1, 1), :]
        heads = range(GDN_HEADS)
        hsl = [slice(h * GDN_DK, (h + 1) * GDN_DK) for h in heads]
        s_prev = [s_scr[h][...] for h in heads]
        v_new = [us[rows, hsl[h]] - _mm(ws[rows, hsl[h]], s_prev[h], NN, GDN_MM_STATE) for h in heads]
        s_add = [_mm(kds[rows, hsl[h]], v_new[h], TN, GDN_MM_STATE) for h in heads]
        for h in heads:
            s_scr[h][...] = s_prev[h] * jnp.exp(g_last_all[:, 12 + h:13 + h]) + s_add[h]
        o = [_mm(qes[rows, hsl[h]], s_prev[h], NN, GDN_MM_STATE) + _mm(qks[h, rows, :], v_new[h], NN, GDN_MM_STATE)
             for h in heads]
        for h in heads:
            og = o[h] * lax.rsqrt(jnp.mean(o[h] * o[h], -1, keepdims=True) + NORM_EPS) * nw_ref[...]
            if ti < tt:
                o_ref[:, hsl[h]] = og[:ti] * _silu(z_ref[:, hsl[h]])
            else:
                o_ref[rows, hsl[h]] = og * _silu(z_ref[rows, hsl[h]])
        return carry

    lax.fori_loop(0, ncb, state_step, 0)

    @pl.when(i == n_i - 1)
    def _():
        for h in range(GDN_HEADS):
            sT_ref[0, h] = s_scr[h][...]


def _gdn(qkv, z, small, gates_t, conv_w, a_log, dt_bias, norm_w, conv0, s0, B, T):
    chunk = GDN_CHUNK
    ti = min(ROW_TILE, T)
    tt = max(ti, chunk)
    n_i = T // ti
    ncb = tt // chunk
    pad = (0, LANES - 16)
    alr = jnp.pad(jnp.concatenate([jnp.zeros((12,), F32), a_log]), pad).reshape(1, LANES)
    dtr = jnp.pad(jnp.concatenate([jnp.zeros((12,), F32), dt_bias]), pad).reshape(1, LANES)
    alc = jnp.concatenate([jnp.zeros((4,), F32), a_log]).reshape(8, 1)
    dtc = jnp.concatenate([jnp.zeros((4,), F32), dt_bias]).reshape(8, 1)
    nw = norm_w.reshape(1, GDN_DV)
    C = GDN_CONV_DIM
    W = GDN_WIDTH

    def const(shape):
        return pl.BlockSpec(shape, lambda b, i: (0,) * len(shape))

    o, s_fin, c_fin = pl.pallas_call(
        functools.partial(_gdn_body, ti=ti, tt=tt, chunk=chunk),
        grid=(B, n_i),
        in_specs=[
            pl.BlockSpec((ti, C), lambda b, i: (b * n_i + i, 0)),
            pl.BlockSpec((ti, W), lambda b, i: (b * n_i + i, 0)),
            pl.BlockSpec((ti, LANES), lambda b, i: (b * n_i + i, 0)),
            pl.BlockSpec((ncb, 8, chunk), lambda b, i: (b * n_i + i, 0, 0)),
            const((CONV_K, C)), const((1, LANES)), const((1, LANES)), const((8, 1)), const((8, 1)),
            const((1, GDN_DV)),
            pl.BlockSpec((1, CONV_K - 1, C), lambda b, i: (b, 0, 0)),
            pl.BlockSpec((1, GDN_HEADS, GDN_DK, GDN_DV), lambda b, i: (b, 0, 0, 0)),
        ],
        out_specs=[
            pl.BlockSpec((ti, W), lambda b, i: (b * n_i + i, 0)),
            pl.BlockSpec((1, GDN_HEADS, GDN_DK, GDN_DV), lambda b, i: (b, 0, 0, 0)),
            pl.BlockSpec((1, CONV_K - 1, C), lambda b, i: (b, 0, 0)),
        ],
        out_shape=[
            jax.ShapeDtypeStruct((B * T, W), F32),
            jax.ShapeDtypeStruct((B, GDN_HEADS, GDN_DK, GDN_DV), F32),
            jax.ShapeDtypeStruct((B, CONV_K - 1, C), F32),
        ],
        scratch_shapes=[
            pltpu.VMEM((ti + 8, C), F32),
            pltpu.VMEM((tt, W), F32), pltpu.VMEM((tt, W), F32), pltpu.VMEM((tt, W), F32),
            pltpu.VMEM((tt, LANES), F32), pltpu.VMEM((tt, LANES), F32),
            pltpu.VMEM((tt, LANES), F32), pltpu.VMEM((ncb, 8, chunk), F32),
            pltpu.VMEM((tt, W), F32), pltpu.VMEM((tt, W), F32),
            pltpu.VMEM((tt, W), F32), pltpu.VMEM((tt, W), F32),
            pltpu.VMEM((GDN_HEADS, tt, chunk), F32),
        ] + [pltpu.VMEM((GDN_DK, GDN_DV), F32)] * GDN_HEADS,
        compiler_params=_cparams(("parallel", "arbitrary")),
        name="gated_delta",
    )(qkv, z, small, gates_t, conv_w, alr, dtr, alc, dtc, nw, conv0, s0)
    return o, s_fin, c_fin


def _gates_transposed(small, B, T):
    chunk = GDN_CHUNK
    g = small[:, IDX_HEADS:IDX_HEADS + 2 * GDN_HEADS].reshape(B, T, 2 * GDN_HEADS)
    tpad = -(-T // chunk) * chunk
    g = jnp.pad(g, ((0, 0), (0, tpad - T), (0, 0)))
    return g.reshape(B * tpad // chunk, chunk, 2 * GDN_HEADS).transpose(0, 2, 1)


def _mixer_body(x_ref, att_ref, gdn_ref, woa_ref, wog_ref, g_ref, b_ref, o_ref):
    mix = (jnp.dot(att_ref[...].astype(BF16), woa_ref[...], preferred_element_type=F32)
           + jnp.dot(gdn_ref[...].astype(BF16), wog_ref[...], preferred_element_type=F32))
    o_ref[...] = _layer_norm_rows(DEEP_ALPHA * x_ref[...] + mix, g_ref[...], b_ref[...])


def _mixer_residual(x, att, gdn, wo_att, wo_gdn, g, b):
    M, D = x.shape
    tm = min(ROW_TILE, M)

    def rows(width):
        return pl.BlockSpec((tm, width), lambda i: (i, 0))

    def full(a):
        return pl.BlockSpec(a.shape, lambda i: (0, 0))

    return pl.pallas_call(
        _mixer_body,
        grid=(M // tm,),
        in_specs=[rows(D), rows(ATT_WIDTH), rows(GDN_WIDTH), full(wo_att), full(wo_gdn), full(g), full(b)],
        out_specs=rows(D),
        out_shape=jax.ShapeDtypeStruct((M, D), F32),
        compiler_params=_cparams(("parallel",)),
        name="mixer_residual",
    )(x, att, gdn, wo_att, wo_gdn, g, b)


def _rope_tables(pos):
    d_rot = HEAD_DIM // 4
    half = d_rot // 2
    inv_freq = ROPE_THETA ** (-jnp.arange(half, dtype=F32) / half)
    ang = pos.astype(F32)[:, None] * inv_freq[None, :]
    cos = jnp.cos(ang)
    sin = jnp.sin(ang)
    n = pos.shape[0]
    ones = jnp.ones((n, HEAD_DIM - d_rot), F32)
    zeros = jnp.zeros((n, HEAD_DIM - d_rot), F32)
    zh = jnp.zeros((n, half), F32)
    c = jnp.concatenate([cos, cos, ones], axis=1)
    sa = jnp.concatenate([-sin, zh, zeros], axis=1)
    sb = jnp.concatenate([zh, sin, zeros], axis=1)
    rep = LANES // HEAD_DIM
    return jnp.tile(c, (1, rep)), jnp.tile(sa, (1, rep)), jnp.tile(sb, (1, rep))


def _split_w_in(w_in):
    pts = [0]
    for s in SPLITS:
        pts.append(pts[-1] + s)
    wq, wk, wv, wqi, wki, wwi, wqkv, wz, wb, wa = [w_in[:, pts[n]:pts[n + 1]] for n in range(len(SPLITS))]
    D = w_in.shape[0]
    wr = jnp.concatenate([wq, wqi, wk, wki, jnp.zeros((D, ROPE_W - 2 * ATT_WIDTH - LANES - IDX_DIM), F32)], axis=1)
    ws = jnp.concatenate([wwi, wb, wa, jnp.zeros((D, LANES - IDX_HEADS - 2 * GDN_HEADS), F32)], axis=1)
    return tuple(a.astype(BF16) for a in (wr, wv, ws, wqkv, wz))


def kernel(x_prompt, x_sample, cache_k, cache_v, cache_kidx, state_gdn, state_conv, page_table, ffn1_w_gate, ffn1_w_up, ffn1_w_down, ln1_g, ln1_b, w_in, conv_w, a_log, dt_bias, gdn_norm_w, w_o, ln2_g, ln2_b, ffn2_w_gate, ffn2_w_up, ffn2_w_down, ln3_g, ln3_b):
    B, S, D = x_prompt.shape
    Bs, T, _ = x_sample.shape
    n_pages = page_table.shape[1]
    page = cache_k.shape[2]
    past = n_pages * page
    kvw = N_KV_HEADS * HEAD_DIM
    l = 0
    xp = x_prompt.reshape(B * S, D)
    xs = x_sample.reshape(Bs * T, D)

    f1 = (ffn1_w_gate[l].astype(BF16), ffn1_w_up[l].astype(BF16), ffn1_w_down[l].astype(BF16), ln1_g, ln1_b)
    f2 = (ffn2_w_gate[l].astype(BF16), ffn2_w_up[l].astype(BF16), ffn2_w_down[l].astype(BF16), ln3_g, ln3_b)
    w_proj = _split_w_in(w_in[l])
    wo_att = w_o[l][:ATT_WIDTH].astype(BF16)
    wo_gdn = w_o[l][ATT_WIDTH:].astype(BF16)

    xp1 = _ffn_half_step(xp, *f1)
    tm_p = min(ROW_TILE, B * S)
    q, qi, kt, kit, vt, small, qkv, z = _mix_projection(xp1, w_proj, _rope_tables(jnp.arange(S)), S // tm_p, seq=S)
    att = _dsa_prompt(q, qi, small, kt, kit, vt)
    gdn, s_p, c_p = _gdn(qkv, z, small, _gates_transposed(small, B, S), conv_w[l], a_log[l], dt_bias[l],
                         gdn_norm_w[l], jnp.zeros((B, CONV_K - 1, GDN_CONV_DIM), F32),
                         jnp.zeros((B, GDN_HEADS, GDN_DK, GDN_DV), F32), B, S)
    xp2 = _mixer_residual(xp1, att, gdn, wo_att, wo_gdn, ln2_g, ln2_b)
    y_prompt = _ffn_half_step(xp2, *f2).reshape(B, S, D)
    k_prompt = kt.reshape(1, B, N_KV_HEADS, HEAD_DIM, S).transpose(0, 1, 4, 2, 3)
    v_prompt = vt.reshape(1, B, N_KV_HEADS, HEAD_DIM, S).transpose(0, 1, 4, 2, 3)
    kidx_prompt = kit.reshape(1, B, IDX_DIM, S).transpose(0, 1, 3, 2)

    xs1 = _ffn_half_step(xs, *f1)
    pos_s = jnp.tile(past + jnp.arange(T), Bs)
    tm_s = min(ROW_TILE, Bs * T)
    q, qi, k, ki, v, small, qkv, z = _mix_projection(xs1, w_proj, _rope_tables(pos_s), Bs * T // tm_s)
    group = ATT_HEADS // N_KV_HEADS
    qh = q.reshape(Bs, T, N_KV_HEADS, group, HEAD_DIM).transpose(0, 2, 3, 1, 4)
    zq = jnp.zeros_like(qh[:, 0])
    q_ht = jnp.concatenate([jnp.concatenate([qh[:, 0], zq], -1), jnp.concatenate([zq, qh[:, 1]], -1)], axis=1)
    q_ht = q_ht.reshape(Bs, ATT_HEADS * T, LANES)
    qi_ht = qi.reshape(Bs, T, IDX_HEADS, IDX_DIM).transpose(0, 2, 1, 3).reshape(Bs, IDX_HEADS * T, IDX_DIM)

    def new_tile(a):
        a = a.reshape(Bs, T, a.shape[-1]).transpose(0, 2, 1)
        return jnp.pad(a, ((0, 0), (0, 0), (0, LANES - T)))

    cki_t = cache_kidx[l].transpose(0, 2, 1)
    ck_t = cache_k[l].transpose(0, 2, 3, 1).reshape(-1, kvw, page)
    cv_t = cache_v[l].transpose(0, 2, 3, 1).reshape(-1, kvw, page)
    o_ht = _dsa_sample(page_table, qi_ht, small.reshape(Bs, T, LANES), q_ht, new_tile(ki), new_tile(k), new_tile(v),
                       cki_t, ck_t, cv_t)
    o_ht = o_ht.reshape(Bs, N_KV_HEADS, group, T, N_KV_HEADS, HEAD_DIM)
    att = jnp.stack([o_ht[:, 0, :, :, 0], o_ht[:, 1, :, :, 1]], axis=1)
    att = att.transpose(0, 3, 1, 2, 4).reshape(Bs * T, ATT_WIDTH)
    gdn, s_s, c_s = _gdn(qkv, z, small, _gates_transposed(small, Bs, T), conv_w[l], a_log[l], dt_bias[l],
                         gdn_norm_w[l], state_conv[l], state_gdn[l], Bs, T)
    xs2 = _mixer_residual(xs1, att, gdn, wo_att, wo_gdn, ln2_g, ln2_b)
    y_sample = _ffn_half_step(xs2, *f2).reshape(Bs, T, D)

    return (y_prompt, y_sample,
            k_prompt, v_prompt, kidx_prompt, s_p[None], c_p[None],
            k.reshape(1, Bs, T, N_KV_HEADS, HEAD_DIM), v.reshape(1, Bs, T, N_KV_HEADS, HEAD_DIM),
            ki.reshape(1, Bs, T, IDX_DIM), s_s[None], c_s[None])
```

```python
import functools
import math

import jax
import jax.numpy as jnp
from jax import lax
from jax.experimental import pallas as pl
from jax.experimental.pallas import tpu as pltpu

F32 = jnp.float32
BF16 = jnp.bfloat16
HI = lax.Precision.HIGHEST

LANES = 128
VMEM_LIMIT = 56 * 1024 * 1024

D_MODEL = 1024
DEPTH = 1
ATT_HEADS = 8
N_KV_HEADS = 2
HEAD_DIM = 64
ATT_WIDTH = ATT_HEADS * HEAD_DIM
ROPE_THETA = 500000.0
IDX_HEADS = 8
IDX_DIM = 64
TOPK_MAX = 256
GDN_HEADS = 4
GDN_DK = 128
GDN_DV = 128
GDN_WIDTH = GDN_HEADS * GDN_DV
GDN_CONV_DIM = 2 * GDN_HEADS * GDN_DK + GDN_WIDTH
CONV_K = 4
GDN_CHUNK = 64
LN_EPS = 1e-5
NORM_EPS = 1e-6
DEEP_ALPHA = (2 * DEPTH) ** 0.25
SPLITS = (ATT_WIDTH, N_KV_HEADS * HEAD_DIM, N_KV_HEADS * HEAD_DIM, IDX_HEADS * IDX_DIM, IDX_DIM, IDX_HEADS,
          GDN_CONV_DIM, GDN_WIDTH, GDN_HEADS, GDN_HEADS)
ROPE_W = 1280
TOPK_SEARCH_BINARY = (20, 2)
TOPK_SEARCH_WIDE = (10, 4)
TOPK_WIDE_MAX_ELEMS = 128 * 1024

NEG_INF = float("-inf")
POS_INF = float("inf")
NN = (((1,), (0,)), ((), ()))
NT = (((1,), (1,)), ((), ()))
TN = (((0,), (0,)), ((), ()))
ROW_TILE = 512
QUERY_TILE = 128
IDX_KEY_CHUNK = 256
CAUSAL_CLASSES = 8
GDN_MM_GRAM = "bf16"
GDN_MM_SOLVE = "x3"
GDN_SOLVE_FINE_LEVELS = 2
GDN_CHUNKS_PER_STEP = 4
GDN_MM_STATE = "bf16"


def _cparams(sem):
    return pltpu.CompilerParams(dimension_semantics=sem, vmem_limit_bytes=VMEM_LIMIT)


def _layer_norm_rows(y, g, b):
    mu = jnp.mean(y, axis=-1, keepdims=True)
    d = y - mu
    var = jnp.mean(d * d, axis=-1, keepdims=True)
    return d * lax.rsqrt(var + LN_EPS) * g + b


def _silu(x):
    return x * jax.nn.sigmoid(x)


def _softplus(x):
    return jnp.maximum(x, 0.0) + jnp.log(1.0 + jnp.exp(-jnp.abs(x)))


def _split_bf16(x):
    hi = x.astype(BF16)
    return hi, (x - hi.astype(F32)).astype(BF16)


def _mm(a, b, dims, mode):
    if mode == "f32":
        return lax.dot_general(a, b, dims, precision=HI, preferred_element_type=F32)
    dot = functools.partial(lax.dot_general, dimension_numbers=dims, preferred_element_type=F32)
    if mode == "bf16":
        return dot(a.astype(BF16), b.astype(BF16))
    ah, al = _split_bf16(a)
    bh, bl = _split_bf16(b)
    return dot(ah, bh) + (dot(ah, bl) + dot(al, bh))


def _ffn_body(*refs, mixed):
    j = pl.program_id(1)
    if mixed:
        (x_in_ref, att_ref, gdn_ref, woa_ref, wog_ref, g2_ref, b2_ref,
         wg_ref, wu_ref, wd_ref, g_ref, b_ref, o_ref, acc_ref, x_ref) = refs

        @pl.when(j == 0)
        def _():
            mix = (jnp.dot(att_ref[...].astype(BF16), woa_ref[...], preferred_element_type=F32)
                   + jnp.dot(gdn_ref[...].astype(BF16), wog_ref[...], preferred_element_type=F32))
            x_ref[...] = _layer_norm_rows(DEEP_ALPHA * x_in_ref[...] + mix, g2_ref[...], b2_ref[...])
    else:
        x_ref, wg_ref, wu_ref, wd_ref, g_ref, b_ref, o_ref, acc_ref = refs
    xb = x_ref[...].astype(BF16)
    hg = jnp.dot(xb, wg_ref[...], preferred_element_type=F32)
    hu = jnp.dot(xb, wu_ref[...], preferred_element_type=F32)
    h = _silu(hg) * hu
    part = jnp.dot(h.astype(BF16), wd_ref[...], preferred_element_type=F32)

    @pl.when(j == 0)
    def _():
        acc_ref[...] = part

    @pl.when(j > 0)
    def _():
        acc_ref[...] += part

    @pl.when(j == pl.num_programs(1) - 1)
    def _():
        y = DEEP_ALPHA * x_ref[...] + 0.5 * acc_ref[...]
        o_ref[...] = _layer_norm_rows(y, g_ref[...], b_ref[...])


def _ffn_half_step(x, wg, wu, wd, g, b, mix=None):
    M, D = x.shape
    FF = wg.shape[1]
    tm = min(ROW_TILE, M)
    tf = FF // 2 if (FF // 2) % LANES == 0 else FF

    def rows(width):
        return pl.BlockSpec((tm, width), lambda i, j: (i, 0))

    def full(a):
        return pl.BlockSpec(a.shape, lambda i, j: (0, 0))

    ffn_specs = [
        pl.BlockSpec((D, tf), lambda i, j: (0, j)),
        pl.BlockSpec((D, tf), lambda i, j: (0, j)),
        pl.BlockSpec((tf, D), lambda i, j: (j, 0)),
        full(g), full(b),
    ]
    scratch = [pltpu.VMEM((tm, D), F32)]
    if mix is None:
        operands = (x, wg, wu, wd, g, b)
        in_specs = [rows(D)] + ffn_specs
    else:
        att, gdn, wo_att, wo_gdn, g2, b2 = mix
        operands = (x, att, gdn, wo_att, wo_gdn, g2, b2, wg, wu, wd, g, b)
        in_specs = [rows(D), rows(att.shape[1]), rows(gdn.shape[1]), full(wo_att), full(wo_gdn), full(g2),
                    full(b2)] + ffn_specs
        scratch.append(pltpu.VMEM((tm, D), F32))
    return pl.pallas_call(
        functools.partial(_ffn_body, mixed=mix is not None),
        grid=(M // tm, FF // tf),
        in_specs=in_specs,
        out_specs=rows(D),
        out_shape=jax.ShapeDtypeStruct((M, D), F32),
        scratch_shapes=scratch,
        compiler_params=_cparams(("parallel", "arbitrary")),
        name="ffn_half_step",
    )(*operands)


def _proj_body(x_ref, wr_ref, wv_ref, ws_ref, wqkv_ref, wz_ref, cos_ref, sa_ref, sb_ref,
               q_ref, qi_ref, k_ref, ki_ref, v_ref, small_ref, qkv_ref, z_ref, *bf16_refs, transposed):
    xb = x_ref[...].astype(BF16)
    r = jnp.dot(xb, wr_ref[...], preferred_element_type=F32)
    c = cos_ref[...]
    sa = sa_ref[...]
    sb = sb_ref[...]

    def rope(slab):
        return slab * c + pltpu.roll(slab, LANES - 8, 1) * sa + pltpu.roll(slab, 8, 1) * sb

    for s in range(ATT_WIDTH // LANES):
        q_ref[:, s * LANES:(s + 1) * LANES] = rope(r[:, s * LANES:(s + 1) * LANES])
    off = ATT_WIDTH
    for s in range(IDX_HEADS * IDX_DIM // LANES):
        qi_ref[:, s * LANES:(s + 1) * LANES] = rope(r[:, off + s * LANES:off + (s + 1) * LANES])
    off += IDX_HEADS * IDX_DIM
    kr = rope(r[:, off:off + LANES])
    kir = rope(r[:, off + LANES:off + 2 * LANES])
    v = jnp.dot(xb, wv_ref[...], preferred_element_type=F32)
    if transposed:
        kb_ref, kib_ref, vb_ref = bf16_refs
        kt = kr.T
        kit = kir.T[:IDX_DIM, :]
        vt = v.T
        k_ref[0] = kt
        ki_ref[0] = kit
        v_ref[0] = vt
        kb_ref[0] = kt.astype(BF16)
        kib_ref[0] = kit.astype(BF16)
        vb_ref[0] = vt.astype(BF16)
    else:
        k_ref[...] = kr
        ki_ref[...] = kir[:, :IDX_DIM]
        v_ref[...] = v
    sm = jnp.dot(xb, ws_ref[...], preferred_element_type=F32)
    lane = lax.broadcasted_iota(jnp.int32, sm.shape, 1)
    small_ref[...] = jnp.where(lane < IDX_HEADS, sm * (IDX_HEADS ** -0.5), sm)
    qkv_ref[...] = jnp.dot(xb, wqkv_ref[...], preferred_element_type=F32)
    z_ref[...] = jnp.dot(xb, wz_ref[...], preferred_element_type=F32)


def _mix_projection(x, w, tabs, n_tab_blocks, seq=None):
    M, D = x.shape
    tm = min(ROW_TILE, M)
    wr, wv, ws, wqkv, wz = w
    cos_t, sa_t, sb_t = tabs
    kvw = N_KV_HEADS * HEAD_DIM

    def full(a):
        return pl.BlockSpec(a.shape, lambda i: (0, 0))

    def rows(width):
        return pl.BlockSpec((tm, width), lambda i: (i, 0))

    tab_spec = pl.BlockSpec((tm, LANES), lambda i: (i % n_tab_blocks, 0))
    if seq is None:
        kv_shapes = [jax.ShapeDtypeStruct((M, kvw), F32), jax.ShapeDtypeStruct((M, IDX_DIM), F32),
                     jax.ShapeDtypeStruct((M, kvw), F32)]
        kv_specs = [rows(kvw), rows(IDX_DIM), rows(kvw)]
    else:
        nt = seq // tm
        nb = M // seq

        def tr(width):
            return pl.BlockSpec((1, width, tm), lambda i: (i // nt, 0, i % nt))

        kv_shapes = [jax.ShapeDtypeStruct((nb, kvw, seq), F32), jax.ShapeDtypeStruct((nb, IDX_DIM, seq), F32),
                     jax.ShapeDtypeStruct((nb, kvw, seq), F32)]
        kv_specs = [tr(kvw), tr(IDX_DIM), tr(kvw)]
    out_shapes = [
        jax.ShapeDtypeStruct((M, ATT_WIDTH), F32),
        jax.ShapeDtypeStruct((M, IDX_HEADS * IDX_DIM), F32),
    ] + kv_shapes + [
        jax.ShapeDtypeStruct((M, LANES), F32),
        jax.ShapeDtypeStruct((M, GDN_CONV_DIM), F32),
        jax.ShapeDtypeStruct((M, GDN_WIDTH), F32),
    ]
    out_specs = [rows(ATT_WIDTH), rows(IDX_HEADS * IDX_DIM)] + kv_specs + [
        rows(LANES), rows(GDN_CONV_DIM), rows(GDN_WIDTH)]
    if seq is not None:
        out_shapes += [jax.ShapeDtypeStruct(s.shape, BF16) for s in kv_shapes]
        out_specs += kv_specs
    return pl.pallas_call(
        functools.partial(_proj_body, transposed=seq is not None),
        grid=(M // tm,),
        in_specs=[rows(D), full(wr), full(wv), full(ws), full(wqkv), full(wz), tab_spec, tab_spec, tab_spec],
        out_specs=out_specs,
        out_shape=out_shapes,
        compiler_params=_cparams(("parallel",)),
        name="mix_projection",
    )(x, wr, wv, ws, wqkv, wz, cos_t, sa_t, sb_t)


def _count_ge(xm, thr):
    return jnp.sum(jnp.where(xm >= thr, 1.0, 0.0), axis=-1, keepdims=True)


def _prefix_count(eq_f32):
    R, L = eq_f32.shape
    ri = lax.broadcasted_iota(jnp.int32, (LANES, LANES), 0)
    ci = lax.broadcasted_iota(jnp.int32, (LANES, LANES), 1)
    tri = jnp.where(ri <= ci, 1.0, 0.0).astype(BF16)
    offset = jnp.zeros((R, 1), F32)
    pieces = []
    for blk in range(L // LANES):
        e = eq_f32[:, blk * LANES:(blk + 1) * LANES].astype(BF16)
        loc = jnp.dot(e, tri, preferred_element_type=F32)
        pieces.append(loc + offset)
        offset = offset + loc[:, LANES - 1:LANES]
    return jnp.concatenate(pieces, axis=1)


def _topk_bias(xms, k, n_steps, arity, bounds=None):
    kf = float(k)
    n = len(xms)
    grp = range(n)
    if bounds is None:
        rowmax = [jnp.max(xm, axis=-1, keepdims=True) for xm in xms]
        lo0 = [jnp.min(jnp.where(xm == NEG_INF, POS_INF, xm), axis=-1, keepdims=True) for xm in xms]
    else:
        lo0 = [b[0] for b in bounds]
        rowmax = [b[1] for b in bounds]
    hi0 = [m + jnp.abs(m) * (2.0 ** -10) + 1.0 for m in rowmax]

    def search(_, c):
        lo, hi = c[:n], c[n:]
        mids = [[lo[g] + (hi[g] - lo[g]) * (j / arity) for j in range(1, arity)] for g in grp]
        ge = [[_count_ge(xms[g], m) >= kf for m in mids[g]] for g in grp]
        new_lo, new_hi = [], []
        for g in grp:
            nlo, nhi = lo[g], hi[g]
            for j in range(arity - 1):
                nlo = jnp.where(ge[g][j], mids[g][j], nlo)
            for j in reversed(range(arity - 1)):
                nhi = jnp.where(ge[g][j], nhi, mids[g][j])
            new_lo.append(nlo)
            new_hi.append(nhi)
        return tuple(new_lo) + tuple(new_hi)

    c = lax.fori_loop(0, n_steps, search, tuple(lo0) + tuple(hi0))
    lo, hi = list(c[:n]), list(c[n:])
    c_lo = [_count_ge(xms[g], lo[g]) for g in grp]
    pending = [jnp.where(c_lo[g] > kf, 1.0, 0.0) for g in grp]

    def cond(c):
        worst = jnp.max(c[3 * n])
        for g in range(1, n):
            worst = jnp.maximum(worst, jnp.max(c[3 * n + g]))
        return worst > 0.0

    def body(c):
        lo, hi, c_lo, pending = c[:n], c[n:2 * n], c[2 * n:3 * n], c[3 * n:]
        t = [jnp.max(jnp.where(xms[g] < hi[g], xms[g], NEG_INF), axis=-1, keepdims=True) for g in grp]
        ct = [_count_ge(xms[g], t[g]) for g in grp]
        out = [[], [], [], []]
        for g in grp:
            hit = ct[g] >= kf
            live = pending[g] > 0.0
            upd = jnp.logical_and(live, hit)
            out[0].append(jnp.where(upd, t[g], lo[g]))
            out[1].append(jnp.where(jnp.logical_and(live, jnp.logical_not(hit)), t[g], hi[g]))
            out[2].append(jnp.where(upd, ct[g], c_lo[g]))
            out[3].append(jnp.where(hit, 0.0, pending[g]))
        return tuple(out[0]) + tuple(out[1]) + tuple(out[2]) + tuple(out[3])

    c = lax.while_loop(cond, body, tuple(lo) + tuple(hi) + tuple(c_lo) + tuple(pending))
    lo, c_lo = c[:n], c[2 * n:3 * n]

    def select(xm, lo, c_lo):
        def plain():
            return jnp.where(xm >= lo, 0.0, NEG_INF)

        def with_ties():
            gt = xm > lo
            eq = xm == lo
            neg_zero = jnp.logical_and(xm == 0.0, 1.0 / xm < 0.0)
            eq_hi = jnp.logical_and(eq, jnp.logical_not(neg_zero))
            eq_lo = jnp.logical_and(eq, neg_zero)
            room = kf - jnp.sum(jnp.where(gt, 1.0, 0.0), axis=-1, keepdims=True)
            rank_hi = _prefix_count(jnp.where(eq_hi, 1.0, 0.0))
            rank_lo = _prefix_count(jnp.where(eq_lo, 1.0, 0.0)) + rank_hi[:, -1:]
            keep = jnp.logical_or(gt, jnp.logical_or(jnp.logical_and(eq_hi, rank_hi <= room),
                                                     jnp.logical_and(eq_lo, rank_lo <= room)))
            return jnp.where(keep, 0.0, NEG_INF)

        return lax.cond(jnp.max(c_lo) > kf, with_ties, plain)

    return [select(xms[g], lo[g], c_lo[g]) for g in grp]


def _indexer_scores(qi_bf, kit_ref, wi, sk):
    ws = wi * (IDX_DIM ** -0.5)
    pieces = []
    lo_run = hi_run = None
    for kc in range(sk // IDX_KEY_CHUNK):
        kit = kit_ref[0, :, kc * IDX_KEY_CHUNK:(kc + 1) * IDX_KEY_CHUNK]
        acc = None
        for h in range(IDX_HEADS):
            s = jnp.dot(qi_bf[:, h * IDX_DIM:(h + 1) * IDX_DIM], kit, preferred_element_type=F32)
            term = jnp.maximum(s, 0.0) * ws[:, h:h + 1]
            acc = term if acc is None else acc + term
        pieces.append(acc)
        for j in range(IDX_KEY_CHUNK // LANES):
            part = acc[:, j * LANES:(j + 1) * LANES]
            lo_run = part if lo_run is None else jnp.minimum(lo_run, part)
            hi_run = part if hi_run is None else jnp.maximum(hi_run, part)
    return (jnp.concatenate(pieces, axis=1), jnp.min(lo_run, axis=-1, keepdims=True),
            jnp.max(hi_run, axis=-1, keepdims=True))


def _dsa_prompt_block(q_ref, qi_ref, sm_ref, kt_ref, kit_ref, vt_ref, o_ref, i, *, tq, ksel, sk):
    q_pos = i * tq + lax.broadcasted_iota(jnp.int32, (tq, sk), 0)
    k_pos = lax.broadcasted_iota(jnp.int32, (tq, sk), 1)
    score, smin, smax = _indexer_scores(qi_ref[...].astype(BF16), kit_ref, sm_ref[...], sk)
    xm = jnp.where(k_pos <= q_pos, score, NEG_INF)
    search = TOPK_SEARCH_WIDE if tq * sk <= TOPK_WIDE_MAX_ELEMS else TOPK_SEARCH_BINARY
    bias = _topk_bias([xm], ksel, *search, bounds=[(smin, smax)])[0]

    kt = kt_ref[0, :, :sk]
    vt = vt_ref[0, :, :sk]
    qb = (q_ref[...] * (HEAD_DIM ** -0.5)).astype(BF16)
    group = ATT_HEADS // N_KV_HEADS
    lane = lax.broadcasted_iota(jnp.int32, (tq, LANES), 1)
    outs = []
    for h in range(ATT_HEADS):
        kv = h // group
        s = jnp.dot(qb[:, h * HEAD_DIM:(h + 1) * HEAD_DIM], kt[kv * HEAD_DIM:(kv + 1) * HEAD_DIM, :],
                    preferred_element_type=F32) + bias
        m = jnp.max(s, axis=-1, keepdims=True)
        p = jnp.exp(s - m)
        l = jnp.sum(p, axis=-1, keepdims=True)
        outs.append(lax.dot_general(p.astype(BF16), vt, NT, preferred_element_type=F32) / l)
    for pair in range(ATT_HEADS // 2):
        h0, h1 = 2 * pair, 2 * pair + 1
        kv = h0 // group
        a, b = outs[h0], outs[h1]
        if kv == 0:
            b = pltpu.roll(b, HEAD_DIM, 1)
        else:
            a = pltpu.roll(a, HEAD_DIM, 1)
        o_ref[:, pair * LANES:(pair + 1) * LANES] = jnp.where(lane < HEAD_DIM, a, b)


def _dsa_prompt_body(q_ref, qi_ref, sm_ref, kt_ref, kit_ref, vt_ref, o_ref, *, tq, ksel, n_classes):
    i = pl.program_id(1)
    per = kt_ref.shape[2] // tq // n_classes
    for c in range(n_classes):
        @pl.when(jnp.logical_and(i >= c * per, i < (c + 1) * per))
        def _(c=c):
            _dsa_prompt_block(q_ref, qi_ref, sm_ref, kt_ref, kit_ref, vt_ref, o_ref, i,
                              tq=tq, ksel=ksel, sk=(c + 1) * per * tq)


def _dsa_prompt(q, qi, small, kt, kit, vt):
    B, kvw, S = kt.shape
    tq = QUERY_TILE
    nq = S // tq
    ksel = min(TOPK_MAX, S // 4)
    n_classes = CAUSAL_CLASSES if nq % CAUSAL_CLASSES == 0 and (nq // CAUSAL_CLASSES * tq) % IDX_KEY_CHUNK == 0 else 1
    return pl.pallas_call(
        functools.partial(_dsa_prompt_body, tq=tq, ksel=ksel, n_classes=n_classes),
        grid=(B, nq),
        in_specs=[
            pl.BlockSpec((tq, ATT_WIDTH), lambda b, i: (b * nq + i, 0)),
            pl.BlockSpec((tq, IDX_HEADS * IDX_DIM), lambda b, i: (b * nq + i, 0)),
            pl.BlockSpec((tq, LANES), lambda b, i: (b * nq + i, 0)),
            pl.BlockSpec((1, kvw, S), lambda b, i: (b, 0, 0)),
            pl.BlockSpec((1, IDX_DIM, S), lambda b, i: (b, 0, 0)),
            pl.BlockSpec((1, kvw, S), lambda b, i: (b, 0, 0)),
        ],
        out_specs=pl.BlockSpec((tq, ATT_WIDTH), lambda b, i: (b * nq + i, 0)),
        out_shape=jax.ShapeDtypeStruct((B * S, ATT_WIDTH), F32),
        compiler_params=_cparams(("parallel", "arbitrary")),
        name="dsa_prompt",
    )(q, qi, small, kt, kit, vt)


def _dsa_sample_body(pt_ref, qi_ref, wi_ref, q_ref, kin_ref, kn_ref, vn_ref, cki_hbm, ck_hbm, cv_hbm, o_ref,
                     kibuf, kbuf, vbuf, sem, *, n_pages, page, t_new, ksel):
    b = pl.program_id(0)
    nb = pl.num_programs(0)
    slot = lax.rem(b, 2)
    past = n_pages * page
    lp = kibuf.shape[2]

    def page_copies(bb, sl, p):
        pg = pt_ref[bb, p]
        cols = pl.ds(p * page, page)
        return (pltpu.make_async_copy(cki_hbm.at[pg], kibuf.at[sl, :, cols], sem.at[sl, 0]),
                pltpu.make_async_copy(ck_hbm.at[pg], kbuf.at[sl, :, cols], sem.at[sl, 1]),
                pltpu.make_async_copy(cv_hbm.at[pg], vbuf.at[sl, :, cols], sem.at[sl, 2]))

    def start_all(bb, sl):
        for p in range(n_pages):
            for cp in page_copies(bb, sl, p):
                cp.start()

    def wait_all(bb, sl):
        for p in range(n_pages):
            for cp in page_copies(bb, sl, p):
                cp.wait()

    @pl.when(b == 0)
    def _():
        start_all(0, 0)

    @pl.when(b + 1 < nb)
    def _():
        start_all(b + 1, 1 - slot)

    kibuf[slot, :, past:lp] = kin_ref[0]
    kbuf[slot, :, past:lp] = kn_ref[0]
    vbuf[slot, :, past:lp] = vn_ref[0]

    wait_all(b, slot)

    s_idx = jnp.dot(qi_ref[0].astype(BF16), kibuf[slot].astype(BF16),
                    preferred_element_type=F32)
    wi = wi_ref[0] * (IDX_DIM ** -0.5)
    score = None
    for h in range(IDX_HEADS):
        term = jnp.maximum(s_idx[h * t_new:(h + 1) * t_new, :], 0.0) * wi[:, h:h + 1]
        score = term if score is None else score + term
    q_pos = past + lax.broadcasted_iota(jnp.int32, (t_new, lp), 0)
    k_pos = lax.broadcasted_iota(jnp.int32, (t_new, lp), 1)
    bias = _topk_bias([jnp.where(k_pos <= q_pos, score, NEG_INF)], ksel, *TOPK_SEARCH_WIDE)[0]

    qb = (q_ref[0] * (HEAD_DIM ** -0.5)).astype(BF16)
    s = jnp.dot(qb, kbuf[slot].astype(BF16), preferred_element_type=F32)
    s = s + jnp.concatenate([bias] * ATT_HEADS, axis=0)
    m = jnp.max(s, axis=-1, keepdims=True)
    p = jnp.exp(s - m)
    l = jnp.sum(p, axis=-1, keepdims=True)
    o_ref[0] = lax.dot_general(p.astype(BF16), vbuf[slot].astype(BF16), NT, preferred_element_type=F32) / l


def _dsa_sample(page_table, qi_ht, wi, q_ht, kit_new, kt_new, vt_new, cki_t, ck_t, cv_t):
    B, n_pages = page_table.shape
    page = cki_t.shape[2]
    t_new = wi.shape[1]
    past = n_pages * page
    lp = past + LANES
    ksel = min(TOPK_MAX, (past + t_new) // 4)
    rows = ATT_HEADS * t_new
    kvw = N_KV_HEADS * HEAD_DIM
    grid_spec = pltpu.PrefetchScalarGridSpec(
        num_scalar_prefetch=1,
        grid=(B,),
        in_specs=[
            pl.BlockSpec((1, rows, IDX_DIM), lambda b, pt: (b, 0, 0)),
            pl.BlockSpec((1, t_new, LANES), lambda b, pt: (b, 0, 0)),
            pl.BlockSpec((1, rows, LANES), lambda b, pt: (b, 0, 0)),
            pl.BlockSpec((1, IDX_DIM, LANES), lambda b, pt: (b, 0, 0)),
            pl.BlockSpec((1, kvw, LANES), lambda b, pt: (b, 0, 0)),
            pl.BlockSpec((1, kvw, LANES), lambda b, pt: (b, 0, 0)),
            pl.BlockSpec(memory_space=pl.ANY),
            pl.BlockSpec(memory_space=pl.ANY),
            pl.BlockSpec(memory_space=pl.ANY),
        ],
        out_specs=pl.BlockSpec((1, rows, LANES), lambda b, pt: (b, 0, 0)),
        scratch_shapes=[
            pltpu.VMEM((2, IDX_DIM, lp), F32),
            pltpu.VMEM((2, kvw, lp), F32),
            pltpu.VMEM((2, kvw, lp), F32),
            pltpu.SemaphoreType.DMA((2, 3)),
        ],
    )
    return pl.pallas_call(
        functools.partial(_dsa_sample_body, n_pages=n_pages, page=page, t_new=t_new, ksel=ksel),
        grid_spec=grid_spec,
        out_shape=jax.ShapeDtypeStruct((B, rows, LANES), F32),
        compiler_params=_cparams(("arbitrary",)),
        name="dsa_sample",
    )(page_table, qi_ht, wi, q_ht, kit_new, kt_new, vt_new, cki_t, ck_t, cv_t)


def _gdn_body(qkv_ref, z_ref, sm_ref, gt_ref, cw_ref, alr_ref, dtr_ref, alc_ref, dtc_ref, nw_ref, c0_ref, s0_ref,
              o_ref, sT_ref, cT_ref, xbuf, qs, ks, vs, bcol, gcol, gcc, gcr, us, ws, qes, kds, qks,
              s0_scr, s1_scr, s2_scr, s3_scr, *, ti, tt, chunk):
    i = pl.program_id(1)
    n_i = pl.num_programs(1)
    nq = GDN_HEADS * GDN_DK
    ncb = tt // chunk
    halo = 8
    s_scr = (s0_scr, s1_scr, s2_scr, s3_scr)

    @pl.when(i == 0)
    def _():
        for h in range(GDN_HEADS):
            s_scr[h][...] = s0_ref[0, h]
        xbuf[halo - (CONV_K - 1):halo, :] = c0_ref[0]

    xbuf[halo:halo + ti, :] = qkv_ref[...]
    new_tail = xbuf[halo + ti - (CONV_K - 1):halo + ti, :]
    cw = cw_ref[...]
    y = None
    for j in range(CONV_K):
        term = xbuf[halo - (CONV_K - 1) + j:halo - (CONV_K - 1) + j + ti, :] * cw[j:j + 1, :]
        y = term if y is None else y + term
    y = _silu(y)
    xbuf[halo - (CONV_K - 1):halo, :] = new_tail

    @pl.when(i == n_i - 1)
    def _():
        cT_ref[0] = new_tail

    if ti < tt:
        zpad = jnp.zeros((tt - ti, nq), F32)
        qs[ti:tt, :] = zpad
        ks[ti:tt, :] = zpad
        vs[ti:tt, :] = zpad
        bcol[ti:tt, :] = zpad[:, :LANES]
        gcol[ti:tt, :] = zpad[:, :LANES]
    for h in range(GDN_HEADS):
        qh = y[:, h * GDN_DK:(h + 1) * GDN_DK]
        kh = y[:, nq + h * GDN_DK:nq + (h + 1) * GDN_DK]
        qs[0:ti, h * GDN_DK:(h + 1) * GDN_DK] = (qh * lax.rsqrt(jnp.sum(qh * qh, -1, keepdims=True) + NORM_EPS)
                                                 * (GDN_DK ** -0.5))
        ks[0:ti, h * GDN_DK:(h + 1) * GDN_DK] = kh * lax.rsqrt(jnp.sum(kh * kh, -1, keepdims=True) + NORM_EPS)
    vs[0:ti, :] = y[:, 2 * nq:]
    sm = sm_ref[...]
    bcol[0:ti, :] = jax.nn.sigmoid(sm)
    gcol[0:ti, :] = -jnp.exp(alr_ref[...]) * _softplus(sm + dtr_ref[...])

    ri = lax.broadcasted_iota(jnp.int32, (chunk, chunk), 0)
    ci = lax.broadcasted_iota(jnp.int32, (chunk, chunk), 1)
    causal = ri >= ci
    strict = ri > ci
    ltri = jnp.where(causal, 1.0, 0.0)
    utri = jnp.where(ri <= ci, 1.0, 0.0)
    col_live = lax.broadcasted_iota(jnp.int32, (8, chunk), 1) < ti
    n_levels = int(math.log2(chunk))

    for c in range(ncb):
        gcc[c * chunk:(c + 1) * chunk, :] = _mm(ltri, gcol[c * chunk:(c + 1) * chunk, :], NN, "f32")
        g_rows = -jnp.exp(alc_ref[...]) * _softplus(gt_ref[c] + dtc_ref[...])
        gcr[c] = _mm(jnp.where(col_live, g_rows, 0.0), utri, NN, "f32")

    cpi = GDN_CHUNKS_PER_STEP if ncb % GDN_CHUNKS_PER_STEP == 0 else 1

    def factor_step(c, carry):
        prob = range(cpi * GDN_HEADS)
        hd = [p % GDN_HEADS for p in prob]
        rows = [pl.ds(pl.multiple_of((c * cpi + p // GDN_HEADS) * chunk, chunk), chunk) for p in prob]
        hsl = [slice(h * GDN_DK, (h + 1) * GDN_DK) for h in hd]
        gcum_col = [gcc[rows[j * GDN_HEADS], :] for j in range(cpi)]
        gcum_row = [gcr[c * cpi + j] for j in range(cpi)]
        b_all = [bcol[rows[j * GDN_HEADS], :] for j in range(cpi)]
        kc = [ks[rows[p], hsl[p]] for p in prob]
        qc = [qs[rows[p], hsl[p]] for p in prob]
        beta = [b_all[p // GDN_HEADS][:, 8 + hd[p]:9 + hd[p]] for p in prob]
        gc = [gcum_col[p // GDN_HEADS][:, 12 + hd[p]:13 + hd[p]] for p in prob]
        eg = [jnp.exp(gc[p]) for p in prob]
        decay = [jnp.exp(jnp.where(causal, gc[p] - gcum_row[p // GDN_HEADS][4 + hd[p]:5 + hd[p], :], NEG_INF))
                 for p in prob]
        kb = [kc[p] * beta[p] for p in prob]
        gram = [_mm(jnp.concatenate([kb[p], qc[p]], axis=0), kc[p], NT, GDN_MM_GRAM) for p in prob]
        a = [-jnp.where(strict, gram[p][:chunk] * decay[p], 0.0) for p in prob]
        sol = [jnp.concatenate([vs[rows[p], hsl[p]] * beta[p], kb[p] * eg[p]], axis=1) for p in prob]
        for lvl in range(n_levels):
            mode = GDN_MM_SOLVE if lvl < GDN_SOLVE_FINE_LEVELS else "bf16"
            sol = [sol[p] + _mm(a[p], sol[p], NN, mode) for p in prob]
            if lvl + 1 < n_levels:
                a = [_mm(a[p], a[p], NN, "bf16") for p in prob]
        for p in prob:
            qks[hd[p], rows[p], :] = jnp.where(causal, gram[p][chunk:] * decay[p], 0.0)
            us[rows[p], hsl[p]] = sol[p][:, :GDN_DV]
            ws[rows[p], hsl[p]] = sol[p][:, GDN_DV:]
            qes[rows[p], hsl[p]] = qc[p] * eg[p]
            kds[rows[p], hsl[p]] = kc[p] * jnp.exp(gc[p][chunk - 1:chunk, :] - gc[p])
        return carry

    lax.fori_loop(0, ncb // cpi, factor_step, 0)

    def state_step(c, carry):
        r0 = pl.multiple_of(c * chunk, chunk)
        rows = pl.ds(r0, chunk)
        g_last_all = gcc[pl.ds(r0 + chunk - 1, 1), :]
        heads = range(GDN_HEADS)
        hsl = [slice(h * GDN_DK, (h + 1) * GDN_DK) for h in heads]
        s_prev = [s_scr[h][...] for h in heads]
        v_new = [us[rows, hsl[h]] - _mm(ws[rows, hsl[h]], s_prev[h], NN, GDN_MM_STATE) for h in heads]
        s_add = [_mm(kds[rows, hsl[h]], v_new[h], TN, GDN_MM_STATE) for h in heads]
        for h in heads:
            s_scr[h][...] = s_prev[h] * jnp.exp(g_last_all[:, 12 + h:13 + h]) + s_add[h]
        o = [_mm(qes[rows, hsl[h]], s_prev[h], NN, GDN_MM_STATE) + _mm(qks[h, rows, :], v_new[h], NN, GDN_MM_STATE)
             for h in heads]
        for h in heads:
            og = o[h] * lax.rsqrt(jnp.mean(o[h] * o[h], -1, keepdims=True) + NORM_EPS) * nw_ref[...]
            if ti < tt:
                o_ref[:, hsl[h]] = og[:ti] * _silu(z_ref[:, hsl[h]])
            else:
                o_ref[rows, hsl[h]] = og * _silu(z_ref[rows, hsl[h]])
        return carry

    lax.fori_loop(0, ncb, state_step, 0)

    @pl.when(i == n_i - 1)
    def _():
        for h in range(GDN_HEADS):
            sT_ref[0, h] = s_scr[h][...]


def _gdn(qkv, z, small, gates_t, conv_w, a_log, dt_bias, norm_w, conv0, s0, B, T):
    chunk = GDN_CHUNK
    ti = min(ROW_TILE, T)
    tt = max(ti, chunk)
    n_i = T // ti
    ncb = tt // chunk
    pad = (0, LANES - 16)
    alr = jnp.pad(jnp.concatenate([jnp.zeros((12,), F32), a_log]), pad).reshape(1, LANES)
    dtr = jnp.pad(jnp.concatenate([jnp.zeros((12,), F32), dt_bias]), pad).reshape(1, LANES)
    alc = jnp.concatenate([jnp.zeros((4,), F32), a_log]).reshape(8, 1)
    dtc = jnp.concatenate([jnp.zeros((4,), F32), dt_bias]).reshape(8, 1)
    nw = norm_w.reshape(1, GDN_DV)
    C = GDN_CONV_DIM
    W = GDN_WIDTH

    def const(shape):
        return pl.BlockSpec(shape, lambda b, i: (0,) * len(shape))

    o, s_fin, c_fin = pl.pallas_call(
        functools.partial(_gdn_body, ti=ti, tt=tt, chunk=chunk),
        grid=(B, n_i),
        in_specs=[
            pl.BlockSpec((ti, C), lambda b, i: (b * n_i + i, 0)),
            pl.BlockSpec((ti, W), lambda b, i: (b * n_i + i, 0)),
            pl.BlockSpec((ti, LANES), lambda b, i: (b * n_i + i, 0)),
            pl.BlockSpec((ncb, 8, chunk), lambda b, i: (b * n_i + i, 0, 0)),
            const((CONV_K, C)), const((1, LANES)), const((1, LANES)), const((8, 1)), const((8, 1)),
            const((1, GDN_DV)),
            pl.BlockSpec((1, CONV_K - 1, C), lambda b, i: (b, 0, 0)),
            pl.BlockSpec((1, GDN_HEADS, GDN_DK, GDN_DV), lambda b, i: (b, 0, 0, 0)),
        ],
        out_specs=[
            pl.BlockSpec((ti, W), lambda b, i: (b * n_i + i, 0)),
            pl.BlockSpec((1, GDN_HEADS, GDN_DK, GDN_DV), lambda b, i: (b, 0, 0, 0)),
            pl.BlockSpec((1, CONV_K - 1, C), lambda b, i: (b, 0, 0)),
        ],
        out_shape=[
            jax.ShapeDtypeStruct((B * T, W), F32),
            jax.ShapeDtypeStruct((B, GDN_HEADS, GDN_DK, GDN_DV), F32),
            jax.ShapeDtypeStruct((B, CONV_K - 1, C), F32),
        ],
        scratch_shapes=[
            pltpu.VMEM((ti + 8, C), F32),
            pltpu.VMEM((tt, W), F32), pltpu.VMEM((tt, W), F32), pltpu.VMEM((tt, W), F32),
            pltpu.VMEM((tt, LANES), F32), pltpu.VMEM((tt, LANES), F32),
            pltpu.VMEM((tt, LANES), F32), pltpu.VMEM((ncb, 8, chunk), F32),
            pltpu.VMEM((tt, W), F32), pltpu.VMEM((tt, W), F32),
            pltpu.VMEM((tt, W), F32), pltpu.VMEM((tt, W), F32),
            pltpu.VMEM((GDN_HEADS, tt, chunk), F32),
        ] + [pltpu.VMEM((GDN_DK, GDN_DV), F32)] * GDN_HEADS,
        compiler_params=_cparams(("parallel", "arbitrary")),
        name="gated_delta",
    )(qkv, z, small, gates_t, conv_w, alr, dtr, alc, dtc, nw, conv0, s0)
    return o, s_fin, c_fin


def _gates_transposed(small, B, T):
    chunk = GDN_CHUNK
    g = small[:, IDX_HEADS:IDX_HEADS + 2 * GDN_HEADS].reshape(B, T, 2 * GDN_HEADS)
    tpad = -(-T // chunk) * chunk
    g = jnp.pad(g, ((0, 0), (0, tpad - T), (0, 0)))
    return g.reshape(B * tpad // chunk, chunk, 2 * GDN_HEADS).transpose(0, 2, 1)


def _rope_tables(pos):
    d_rot = HEAD_DIM // 4
    half = d_rot // 2
    inv_freq = ROPE_THETA ** (-jnp.arange(half, dtype=F32) / half)
    ang = pos.astype(F32)[:, None] * inv_freq[None, :]
    cos = jnp.cos(ang)
    sin = jnp.sin(ang)
    n = pos.shape[0]
    ones = jnp.ones((n, HEAD_DIM - d_rot), F32)
    zeros = jnp.zeros((n, HEAD_DIM - d_rot), F32)
    zh = jnp.zeros((n, half), F32)
    c = jnp.concatenate([cos, cos, ones], axis=1)
    sa = jnp.concatenate([-sin, zh, zeros], axis=1)
    sb = jnp.concatenate([zh, sin, zeros], axis=1)
    rep = LANES // HEAD_DIM
    return jnp.tile(c, (1, rep)), jnp.tile(sa, (1, rep)), jnp.tile(sb, (1, rep))


def _split_w_in(w_in):
    pts = [0]
    for s in SPLITS:
        pts.append(pts[-1] + s)
    wq, wk, wv, wqi, wki, wwi, wqkv, wz, wb, wa = [w_in[:, pts[n]:pts[n + 1]] for n in range(len(SPLITS))]
    D = w_in.shape[0]
    wr = jnp.concatenate([wq, wqi, wk, wki, jnp.zeros((D, ROPE_W - 2 * ATT_WIDTH - LANES - IDX_DIM), F32)], axis=1)
    ws = jnp.concatenate([wwi, wb, wa, jnp.zeros((D, LANES - IDX_HEADS - 2 * GDN_HEADS), F32)], axis=1)
    return tuple(a.astype(BF16) for a in (wr, wv, ws, wqkv, wz))


def kernel(x_prompt, x_sample, cache_k, cache_v, cache_kidx, state_gdn, state_conv, page_table, ffn1_w_gate, ffn1_w_up, ffn1_w_down, ln1_g, ln1_b, w_in, conv_w, a_log, dt_bias, gdn_norm_w, w_o, ln2_g, ln2_b, ffn2_w_gate, ffn2_w_up, ffn2_w_down, ln3_g, ln3_b):
    B, S, D = x_prompt.shape
    Bs, T, _ = x_sample.shape
    n_pages = page_table.shape[1]
    page = cache_k.shape[2]
    past = n_pages * page
    kvw = N_KV_HEADS * HEAD_DIM
    l = 0
    xp = x_prompt.reshape(B * S, D)
    xs = x_sample.reshape(Bs * T, D)

    f1 = (ffn1_w_gate[l].astype(BF16), ffn1_w_up[l].astype(BF16), ffn1_w_down[l].astype(BF16), ln1_g, ln1_b)
    f2 = (ffn2_w_gate[l].astype(BF16), ffn2_w_up[l].astype(BF16), ffn2_w_down[l].astype(BF16), ln3_g, ln3_b)
    w_proj = _split_w_in(w_in[l])
    wo_att = w_o[l][:ATT_WIDTH].astype(BF16)
    wo_gdn = w_o[l][ATT_WIDTH:].astype(BF16)

    xp1 = _ffn_half_step(xp, *f1)
    tm_p = min(ROW_TILE, B * S)
    q, qi, kt, kit, vt, small, qkv, z, kt_bf, kit_bf, vt_bf = _mix_projection(
        xp1, w_proj, _rope_tables(jnp.arange(S)), S // tm_p, seq=S)
    att = _dsa_prompt(q, qi, small, kt_bf, kit_bf, vt_bf)
    gdn, s_p, c_p = _gdn(qkv, z, small, _gates_transposed(small, B, S), conv_w[l], a_log[l], dt_bias[l],
                         gdn_norm_w[l], jnp.zeros((B, CONV_K - 1, GDN_CONV_DIM), F32),
                         jnp.zeros((B, GDN_HEADS, GDN_DK, GDN_DV), F32), B, S)
    y_prompt = _ffn_half_step(xp1, *f2, mix=(att, gdn, wo_att, wo_gdn, ln2_g, ln2_b)).reshape(B, S, D)
    k_prompt = kt.reshape(1, B, N_KV_HEADS, HEAD_DIM, S).transpose(0, 1, 4, 2, 3)
    v_prompt = vt.reshape(1, B, N_KV_HEADS, HEAD_DIM, S).transpose(0, 1, 4, 2, 3)
    kidx_prompt = kit.reshape(1, B, IDX_DIM, S).transpose(0, 1, 3, 2)

    xs1 = _ffn_half_step(xs, *f1)
    pos_s = jnp.tile(past + jnp.arange(T), Bs)
    tm_s = min(ROW_TILE, Bs * T)
    q, qi, k, ki, v, small, qkv, z = _mix_projection(xs1, w_proj, _rope_tables(pos_s), Bs * T // tm_s)
    group = ATT_HEADS // N_KV_HEADS
    qh = q.reshape(Bs, T, N_KV_HEADS, group, HEAD_DIM).transpose(0, 2, 3, 1, 4)
    zq = jnp.zeros_like(qh[:, 0])
    q_ht = jnp.concatenate([jnp.concatenate([qh[:, 0], zq], -1), jnp.concatenate([zq, qh[:, 1]], -1)], axis=1)
    q_ht = q_ht.reshape(Bs, ATT_HEADS * T, LANES)
    qi_ht = qi.reshape(Bs, T, IDX_HEADS, IDX_DIM).transpose(0, 2, 1, 3).reshape(Bs, IDX_HEADS * T, IDX_DIM)

    def new_tile(a):
        a = a.reshape(Bs, T, a.shape[-1]).transpose(0, 2, 1)
        return jnp.pad(a, ((0, 0), (0, 0), (0, LANES - T)))

    cki_t = cache_kidx[l].transpose(0, 2, 1)
    ck_t = cache_k[l].transpose(0, 2, 3, 1).reshape(-1, kvw, page)
    cv_t = cache_v[l].transpose(0, 2, 3, 1).reshape(-1, kvw, page)
    o_ht = _dsa_sample(page_table, qi_ht, small.reshape(Bs, T, LANES), q_ht, new_tile(ki), new_tile(k), new_tile(v),
                       cki_t, ck_t, cv_t)
    o_ht = o_ht.reshape(Bs, N_KV_HEADS, group, T, N_KV_HEADS, HEAD_DIM)
    att = jnp.stack([o_ht[:, 0, :, :, 0], o_ht[:, 1, :, :, 1]], axis=1)
    att = att.transpose(0, 3, 1, 2, 4).reshape(Bs * T, ATT_WIDTH)
    gdn, s_s, c_s = _gdn(qkv, z, small, _gates_transposed(small, Bs, T), conv_w[l], a_log[l], dt_bias[l],
                         gdn_norm_w[l], state_conv[l], state_gdn[l], Bs, T)
    y_sample = _ffn_half_step(xs1, *f2, mix=(att, gdn, wo_att, wo_gdn, ln2_g, ln2_b)).reshape(Bs, T, D)

    return (y_prompt, y_sample,
            k_prompt, v_prompt, kidx_prompt, s_p[None], c_p[None],
            k.reshape(1, Bs, T, N_KV_HEADS, HEAD_DIM), v.reshape(1, Bs, T, N_KV_HEADS, HEAD_DIM),
            ki.reshape(1, Bs, T, IDX_DIM), s_s[None], c_s[None])
```

```python
import functools
import math

import jax
import jax.numpy as jnp
from jax import lax
from jax.experimental import pallas as pl
from jax.experimental.pallas import tpu as pltpu

F32 = jnp.float32
BF16 = jnp.bfloat16
HI = lax.Precision.HIGHEST

LANES = 128
VMEM_LIMIT = 56 * 1024 * 1024

D_MODEL = 1024
DEPTH = 1
ATT_HEADS = 8
N_KV_HEADS = 2
HEAD_DIM = 64
ATT_WIDTH = ATT_HEADS * HEAD_DIM
ROPE_THETA = 500000.0
IDX_HEADS = 8
IDX_DIM = 64
TOPK_MAX = 256
GDN_HEADS = 4
GDN_DK = 128
GDN_DV = 128
GDN_WIDTH = GDN_HEADS * GDN_DV
GDN_CONV_DIM = 2 * GDN_HEADS * GDN_DK + GDN_WIDTH
CONV_K = 4
GDN_CHUNK = 64
LN_EPS = 1e-5
NORM_EPS = 1e-6
DEEP_ALPHA = (2 * DEPTH) ** 0.25
SPLITS = (ATT_WIDTH, N_KV_HEADS * HEAD_DIM, N_KV_HEADS * HEAD_DIM, IDX_HEADS * IDX_DIM, IDX_DIM, IDX_HEADS,
          GDN_CONV_DIM, GDN_WIDTH, GDN_HEADS, GDN_HEADS)
ROPE_W = 1280
TOPK_SEARCH_BINARY = (20, 2)
TOPK_SEARCH_WIDE = (10, 4)
TOPK_WIDE_MAX_ELEMS = 128 * 1024

NEG_INF = float("-inf")
POS_INF = float("inf")
NN = (((1,), (0,)), ((), ()))
NT = (((1,), (1,)), ((), ()))
TN = (((0,), (0,)), ((), ()))
ROW_TILE = 512
QUERY_TILE = 128
IDX_KEY_CHUNK = 256
CAUSAL_CLASSES = 4
GDN_MM_GRAM = "bf16"
GDN_MM_SOLVE = "x3"
GDN_SOLVE_FINE_LEVELS = 2
GDN_CHUNKS_PER_STEP = 4
GDN_MM_STATE = "bf16"


def _cparams(sem):
    return pltpu.CompilerParams(dimension_semantics=sem, vmem_limit_bytes=VMEM_LIMIT)


def _layer_norm_rows(y, g, b):
    mu = jnp.mean(y, axis=-1, keepdims=True)
    d = y - mu
    var = jnp.mean(d * d, axis=-1, keepdims=True)
    return d * lax.rsqrt(var + LN_EPS) * g + b


def _silu(x):
    return x * jax.nn.sigmoid(x)


def _softplus(x):
    return jnp.maximum(x, 0.0) + jnp.log(1.0 + jnp.exp(-jnp.abs(x)))


def _split_bf16(x):
    hi = x.astype(BF16)
    return hi, (x - hi.astype(F32)).astype(BF16)


def _mm(a, b, dims, mode):
    if mode == "f32":
        return lax.dot_general(a, b, dims, precision=HI, preferred_element_type=F32)
    dot = functools.partial(lax.dot_general, dimension_numbers=dims, preferred_element_type=F32)
    if mode == "bf16":
        return dot(a.astype(BF16), b.astype(BF16))
    ah, al = _split_bf16(a)
    bh, bl = _split_bf16(b)
    return dot(ah, bh) + (dot(ah, bl) + dot(al, bh))


def _ffn_body(*refs, mixed):
    j = pl.program_id(1)
    if mixed:
        (x_in_ref, att_ref, gdn_ref, woa_ref, wog_ref, g2_ref, b2_ref,
         wg_ref, wu_ref, wd_ref, g_ref, b_ref, o_ref, acc_ref, x_ref) = refs

        @pl.when(j == 0)
        def _():
            mix = (jnp.dot(att_ref[...].astype(BF16), woa_ref[...], preferred_element_type=F32)
                   + jnp.dot(gdn_ref[...].astype(BF16), wog_ref[...], preferred_element_type=F32))
            x_ref[...] = _layer_norm_rows(DEEP_ALPHA * x_in_ref[...] + mix, g2_ref[...], b2_ref[...])
    else:
        x_ref, wg_ref, wu_ref, wd_ref, g_ref, b_ref, o_ref, acc_ref = refs
    xb = x_ref[...].astype(BF16)
    hg = jnp.dot(xb, wg_ref[...], preferred_element_type=F32)
    hu = jnp.dot(xb, wu_ref[...], preferred_element_type=F32)
    h = _silu(hg) * hu
    part = jnp.dot(h.astype(BF16), wd_ref[...], preferred_element_type=F32)

    @pl.when(j == 0)
    def _():
        acc_ref[...] = part

    @pl.when(j > 0)
    def _():
        acc_ref[...] += part

    @pl.when(j == pl.num_programs(1) - 1)
    def _():
        y = DEEP_ALPHA * x_ref[...] + 0.5 * acc_ref[...]
        o_ref[...] = _layer_norm_rows(y, g_ref[...], b_ref[...])


def _ffn_half_step(x, wg, wu, wd, g, b, mix=None):
    M, D = x.shape
    FF = wg.shape[1]
    tm = min(ROW_TILE, M)
    tf = FF // 2 if (FF // 2) % LANES == 0 else FF

    def rows(width):
        return pl.BlockSpec((tm, width), lambda i, j: (i, 0))

    def full(a):
        return pl.BlockSpec(a.shape, lambda i, j: (0, 0))

    ffn_specs = [
        pl.BlockSpec((D, tf), lambda i, j: (0, j)),
        pl.BlockSpec((D, tf), lambda i, j: (0, j)),
        pl.BlockSpec((tf, D), lambda i, j: (j, 0)),
        full(g), full(b),
    ]
    scratch = [pltpu.VMEM((tm, D), F32)]
    if mix is None:
        operands = (x, wg, wu, wd, g, b)
        in_specs = [rows(D)] + ffn_specs
    else:
        att, gdn, wo_att, wo_gdn, g2, b2 = mix
        operands = (x, att, gdn, wo_att, wo_gdn, g2, b2, wg, wu, wd, g, b)
        in_specs = [rows(D), rows(att.shape[1]), rows(gdn.shape[1]), full(wo_att), full(wo_gdn), full(g2),
                    full(b2)] + ffn_specs
        scratch.append(pltpu.VMEM((tm, D), F32))
    return pl.pallas_call(
        functools.partial(_ffn_body, mixed=mix is not None),
        grid=(M // tm, FF // tf),
        in_specs=in_specs,
        out_specs=rows(D),
        out_shape=jax.ShapeDtypeStruct((M, D), F32),
        scratch_shapes=scratch,
        compiler_params=_cparams(("parallel", "arbitrary")),
        name="ffn_half_step",
    )(*operands)


def _proj_body(x_ref, wr_ref, wv_ref, ws_ref, wqkv_ref, wz_ref, cos_ref, sa_ref, sb_ref,
               q_ref, qi_ref, k_ref, ki_ref, v_ref, small_ref, qkv_ref, z_ref, *bf16_refs, transposed):
    xb = x_ref[...].astype(BF16)
    r = jnp.dot(xb, wr_ref[...], preferred_element_type=F32)
    c = cos_ref[...]
    sa = sa_ref[...]
    sb = sb_ref[...]

    def rope(slab):
        return slab * c + pltpu.roll(slab, LANES - 8, 1) * sa + pltpu.roll(slab, 8, 1) * sb

    for s in range(ATT_WIDTH // LANES):
        q_ref[:, s * LANES:(s + 1) * LANES] = rope(r[:, s * LANES:(s + 1) * LANES])
    off = ATT_WIDTH
    for s in range(IDX_HEADS * IDX_DIM // LANES):
        qi_ref[:, s * LANES:(s + 1) * LANES] = rope(r[:, off + s * LANES:off + (s + 1) * LANES])
    off += IDX_HEADS * IDX_DIM
    kr = rope(r[:, off:off + LANES])
    kir = rope(r[:, off + LANES:off + 2 * LANES])
    v = jnp.dot(xb, wv_ref[...], preferred_element_type=F32)
    if transposed:
        kb_ref, kib_ref, vb_ref = bf16_refs
        kt = kr.T
        kit = kir.T[:IDX_DIM, :]
        vt = v.T
        k_ref[0] = kt
        ki_ref[0] = kit
        v_ref[0] = vt
        kb_ref[0] = kt.astype(BF16)
        kib_ref[0] = kit.astype(BF16)
        vb_ref[0] = vt.astype(BF16)
    else:
        k_ref[...] = kr
        ki_ref[...] = kir[:, :IDX_DIM]
        v_ref[...] = v
    sm = jnp.dot(xb, ws_ref[...], preferred_element_type=F32)
    lane = lax.broadcasted_iota(jnp.int32, sm.shape, 1)
    small_ref[...] = jnp.where(lane < IDX_HEADS, sm * (IDX_HEADS ** -0.5), sm)
    qkv_ref[...] = jnp.dot(xb, wqkv_ref[...], preferred_element_type=F32)
    z_ref[...] = jnp.dot(xb, wz_ref[...], preferred_element_type=F32)


def _mix_projection(x, w, tabs, n_tab_blocks, seq=None):
    M, D = x.shape
    tm = min(ROW_TILE, M)
    wr, wv, ws, wqkv, wz = w
    cos_t, sa_t, sb_t = tabs
    kvw = N_KV_HEADS * HEAD_DIM

    def full(a):
        return pl.BlockSpec(a.shape, lambda i: (0, 0))

    def rows(width):
        return pl.BlockSpec((tm, width), lambda i: (i, 0))

    tab_spec = pl.BlockSpec((tm, LANES), lambda i: (i % n_tab_blocks, 0))
    if seq is None:
        kv_shapes = [jax.ShapeDtypeStruct((M, kvw), F32), jax.ShapeDtypeStruct((M, IDX_DIM), F32),
                     jax.ShapeDtypeStruct((M, kvw), F32)]
        kv_specs = [rows(kvw), rows(IDX_DIM), rows(kvw)]
    else:
        nt = seq // tm
        nb = M // seq

        def tr(width):
            return pl.BlockSpec((1, width, tm), lambda i: (i // nt, 0, i % nt))

        kv_shapes = [jax.ShapeDtypeStruct((nb, kvw, seq), F32), jax.ShapeDtypeStruct((nb, IDX_DIM, seq), F32),
                     jax.ShapeDtypeStruct((nb, kvw, seq), F32)]
        kv_specs = [tr(kvw), tr(IDX_DIM), tr(kvw)]
    out_shapes = [
        jax.ShapeDtypeStruct((M, ATT_WIDTH), F32),
        jax.ShapeDtypeStruct((M, IDX_HEADS * IDX_DIM), F32),
    ] + kv_shapes + [
        jax.ShapeDtypeStruct((M, LANES), F32),
        jax.ShapeDtypeStruct((M, GDN_CONV_DIM), F32),
        jax.ShapeDtypeStruct((M, GDN_WIDTH), F32),
    ]
    out_specs = [rows(ATT_WIDTH), rows(IDX_HEADS * IDX_DIM)] + kv_specs + [
        rows(LANES), rows(GDN_CONV_DIM), rows(GDN_WIDTH)]
    if seq is not None:
        out_shapes += [jax.ShapeDtypeStruct(s.shape, BF16) for s in kv_shapes]
        out_specs += kv_specs
    return pl.pallas_call(
        functools.partial(_proj_body, transposed=seq is not None),
        grid=(M // tm,),
        in_specs=[rows(D), full(wr), full(wv), full(ws), full(wqkv), full(wz), tab_spec, tab_spec, tab_spec],
        out_specs=out_specs,
        out_shape=out_shapes,
        compiler_params=_cparams(("parallel",)),
        name="mix_projection",
    )(x, wr, wv, ws, wqkv, wz, cos_t, sa_t, sb_t)


def _count_ge(xm, thr):
    return jnp.sum(jnp.where(xm >= thr, 1.0, 0.0), axis=-1, keepdims=True)


def _prefix_count(eq_f32):
    R, L = eq_f32.shape
    ri = lax.broadcasted_iota(jnp.int32, (LANES, LANES), 0)
    ci = lax.broadcasted_iota(jnp.int32, (LANES, LANES), 1)
    tri = jnp.where(ri <= ci, 1.0, 0.0).astype(BF16)
    offset = jnp.zeros((R, 1), F32)
    pieces = []
    for blk in range(L // LANES):
        e = eq_f32[:, blk * LANES:(blk + 1) * LANES].astype(BF16)
        loc = jnp.dot(e, tri, preferred_element_type=F32)
        pieces.append(loc + offset)
        offset = offset + loc[:, LANES - 1:LANES]
    return jnp.concatenate(pieces, axis=1)


def _topk_bias(xms, k, n_steps, arity, bounds=None):
    kf = float(k)
    n = len(xms)
    grp = range(n)
    if bounds is None:
        rowmax = [jnp.max(xm, axis=-1, keepdims=True) for xm in xms]
        lo0 = [jnp.min(jnp.where(xm == NEG_INF, POS_INF, xm), axis=-1, keepdims=True) for xm in xms]
    else:
        lo0 = [b[0] for b in bounds]
        rowmax = [b[1] for b in bounds]
    hi0 = [m + jnp.abs(m) * (2.0 ** -10) + 1.0 for m in rowmax]

    def search(_, c):
        lo, hi = c[:n], c[n:]
        mids = [[lo[g] + (hi[g] - lo[g]) * (j / arity) for j in range(1, arity)] for g in grp]
        ge = [[_count_ge(xms[g], m) >= kf for m in mids[g]] for g in grp]
        new_lo, new_hi = [], []
        for g in grp:
            nlo, nhi = lo[g], hi[g]
            for j in range(arity - 1):
                nlo = jnp.where(ge[g][j], mids[g][j], nlo)
            for j in reversed(range(arity - 1)):
                nhi = jnp.where(ge[g][j], nhi, mids[g][j])
            new_lo.append(nlo)
            new_hi.append(nhi)
        return tuple(new_lo) + tuple(new_hi)

    c = lax.fori_loop(0, n_steps, search, tuple(lo0) + tuple(hi0))
    lo, hi = list(c[:n]), list(c[n:])
    c_lo = [_count_ge(xms[g], lo[g]) for g in grp]
    pending = [jnp.where(c_lo[g] > kf, 1.0, 0.0) for g in grp]

    def cond(c):
        worst = jnp.max(c[3 * n])
        for g in range(1, n):
            worst = jnp.maximum(worst, jnp.max(c[3 * n + g]))
        return worst > 0.0

    def body(c):
        lo, hi, c_lo, pending = c[:n], c[n:2 * n], c[2 * n:3 * n], c[3 * n:]
        t = [jnp.max(jnp.where(xms[g] < hi[g], xms[g], NEG_INF), axis=-1, keepdims=True) for g in grp]
        ct = [_count_ge(xms[g], t[g]) for g in grp]
        out = [[], [], [], []]
        for g in grp:
            hit = ct[g] >= kf
            live = pending[g] > 0.0
            upd = jnp.logical_and(live, hit)
            out[0].append(jnp.where(upd, t[g], lo[g]))
            out[1].append(jnp.where(jnp.logical_and(live, jnp.logical_not(hit)), t[g], hi[g]))
            out[2].append(jnp.where(upd, ct[g], c_lo[g]))
            out[3].append(jnp.where(hit, 0.0, pending[g]))
        return tuple(out[0]) + tuple(out[1]) + tuple(out[2]) + tuple(out[3])

    c = lax.while_loop(cond, body, tuple(lo) + tuple(hi) + tuple(c_lo) + tuple(pending))
    lo, c_lo = c[:n], c[2 * n:3 * n]

    def select(xm, lo, c_lo):
        def plain():
            return jnp.where(xm >= lo, 0.0, NEG_INF)

        def with_ties():
            gt = xm > lo
            eq = xm == lo
            neg_zero = jnp.logical_and(xm == 0.0, 1.0 / xm < 0.0)
            eq_hi = jnp.logical_and(eq, jnp.logical_not(neg_zero))
            eq_lo = jnp.logical_and(eq, neg_zero)
            room = kf - jnp.sum(jnp.where(gt, 1.0, 0.0), axis=-1, keepdims=True)
            rank_hi = _prefix_count(jnp.where(eq_hi, 1.0, 0.0))
            rank_lo = _prefix_count(jnp.where(eq_lo, 1.0, 0.0)) + rank_hi[:, -1:]
            keep = jnp.logical_or(gt, jnp.logical_or(jnp.logical_and(eq_hi, rank_hi <= room),
                                                     jnp.logical_and(eq_lo, rank_lo <= room)))
            return jnp.where(keep, 0.0, NEG_INF)

        return lax.cond(jnp.max(c_lo) > kf, with_ties, plain)

    return [select(xms[g], lo[g], c_lo[g]) for g in grp]


def _indexer_scores(qi_bf, kit_ref, wi, sk):
    ws = wi * (IDX_DIM ** -0.5)
    pieces = []
    lo_run = hi_run = None
    for kc in range(sk // IDX_KEY_CHUNK):
        kit = kit_ref[0, :, kc * IDX_KEY_CHUNK:(kc + 1) * IDX_KEY_CHUNK]
        acc = None
        for h in range(IDX_HEADS):
            s = jnp.dot(qi_bf[:, h * IDX_DIM:(h + 1) * IDX_DIM], kit, preferred_element_type=F32)
            term = jnp.maximum(s, 0.0) * ws[:, h:h + 1]
            acc = term if acc is None else acc + term
        pieces.append(acc)
        for j in range(IDX_KEY_CHUNK // LANES):
            part = acc[:, j * LANES:(j + 1) * LANES]
            lo_run = part if lo_run is None else jnp.minimum(lo_run, part)
            hi_run = part if hi_run is None else jnp.maximum(hi_run, part)
    return (jnp.concatenate(pieces, axis=1), jnp.min(lo_run, axis=-1, keepdims=True),
            jnp.max(hi_run, axis=-1, keepdims=True))


def _dsa_prompt_block(q_ref, qi_ref, sm_ref, kt_ref, kit_ref, vt_ref, o_ref, i, *, tq, ksel, sk):
    q_pos = i * tq + lax.broadcasted_iota(jnp.int32, (tq, sk), 0)
    k_pos = lax.broadcasted_iota(jnp.int32, (tq, sk), 1)
    score, smin, smax = _indexer_scores(qi_ref[...].astype(BF16), kit_ref, sm_ref[...], sk)
    xm = jnp.where(k_pos <= q_pos, score, NEG_INF)
    search = TOPK_SEARCH_WIDE if tq * sk <= TOPK_WIDE_MAX_ELEMS else TOPK_SEARCH_BINARY
    bias = _topk_bias([xm], ksel, *search, bounds=[(smin, smax)])[0]

    kt = kt_ref[0, :, :sk]
    vt = vt_ref[0, :, :sk]
    qb = (q_ref[...] * (HEAD_DIM ** -0.5)).astype(BF16)
    group = ATT_HEADS // N_KV_HEADS
    lane = lax.broadcasted_iota(jnp.int32, (tq, LANES), 1)
    outs = []
    for h in range(ATT_HEADS):
        kv = h // group
        s = jnp.dot(qb[:, h * HEAD_DIM:(h + 1) * HEAD_DIM], kt[kv * HEAD_DIM:(kv + 1) * HEAD_DIM, :],
                    preferred_element_type=F32) + bias
        m = jnp.max(s, axis=-1, keepdims=True)
        p = jnp.exp(s - m)
        l = jnp.sum(p, axis=-1, keepdims=True)
        outs.append(lax.dot_general(p.astype(BF16), vt, NT, preferred_element_type=F32) / l)
    for pair in range(ATT_HEADS // 2):
        h0, h1 = 2 * pair, 2 * pair + 1
        kv = h0 // group
        a, b = outs[h0], outs[h1]
        if kv == 0:
            b = pltpu.roll(b, HEAD_DIM, 1)
        else:
            a = pltpu.roll(a, HEAD_DIM, 1)
        o_ref[:, pair * LANES:(pair + 1) * LANES] = jnp.where(lane < HEAD_DIM, a, b)


def _dsa_prompt_body(q_ref, qi_ref, sm_ref, kt_ref, kit_ref, vt_ref, o_ref, *, tq, ksel, n_classes):
    i = pl.program_id(1)
    per = kt_ref.shape[2] // tq // n_classes
    for c in range(n_classes):
        @pl.when(jnp.logical_and(i >= c * per, i < (c + 1) * per))
        def _(c=c):
            _dsa_prompt_block(q_ref, qi_ref, sm_ref, kt_ref, kit_ref, vt_ref, o_ref, i,
                              tq=tq, ksel=ksel, sk=(c + 1) * per * tq)


def _dsa_prompt(q, qi, small, kt, kit, vt):
    B, kvw, S = kt.shape
    tq = QUERY_TILE
    nq = S // tq
    ksel = min(TOPK_MAX, S // 4)
    n_classes = CAUSAL_CLASSES if nq % CAUSAL_CLASSES == 0 and (nq // CAUSAL_CLASSES * tq) % IDX_KEY_CHUNK == 0 else 1
    return pl.pallas_call(
        functools.partial(_dsa_prompt_body, tq=tq, ksel=ksel, n_classes=n_classes),
        grid=(B, nq),
        in_specs=[
            pl.BlockSpec((tq, ATT_WIDTH), lambda b, i: (b * nq + i, 0)),
            pl.BlockSpec((tq, IDX_HEADS * IDX_DIM), lambda b, i: (b * nq + i, 0)),
            pl.BlockSpec((tq, LANES), lambda b, i: (b * nq + i, 0)),
            pl.BlockSpec((1, kvw, S), lambda b, i: (b, 0, 0)),
            pl.BlockSpec((1, IDX_DIM, S), lambda b, i: (b, 0, 0)),
            pl.BlockSpec((1, kvw, S), lambda b, i: (b, 0, 0)),
        ],
        out_specs=pl.BlockSpec((tq, ATT_WIDTH), lambda b, i: (b * nq + i, 0)),
        out_shape=jax.ShapeDtypeStruct((B * S, ATT_WIDTH), F32),
        compiler_params=_cparams(("parallel", "arbitrary")),
        name="dsa_prompt",
    )(q, qi, small, kt, kit, vt)


def _dsa_sample_body(pt_ref, qi_ref, wi_ref, q_ref, kin_ref, kn_ref, vn_ref, cki_hbm, ck_hbm, cv_hbm, o_ref,
                     kibuf, kbuf, vbuf, sem, *, n_pages, page, t_new, ksel):
    b = pl.program_id(0)
    nb = pl.num_programs(0)
    slot = lax.rem(b, 2)
    past = n_pages * page
    lp = kibuf.shape[2]

    def page_copies(bb, sl, p):
        pg = pt_ref[bb, p]
        cols = pl.ds(p * page, page)
        return (pltpu.make_async_copy(cki_hbm.at[pg], kibuf.at[sl, :, cols], sem.at[sl, 0]),
                pltpu.make_async_copy(ck_hbm.at[pg], kbuf.at[sl, :, cols], sem.at[sl, 1]),
                pltpu.make_async_copy(cv_hbm.at[pg], vbuf.at[sl, :, cols], sem.at[sl, 2]))

    def start_all(bb, sl):
        for p in range(n_pages):
            for cp in page_copies(bb, sl, p):
                cp.start()

    def wait_all(bb, sl):
        for p in range(n_pages):
            for cp in page_copies(bb, sl, p):
                cp.wait()

    @pl.when(b == 0)
    def _():
        start_all(0, 0)

    @pl.when(b + 1 < nb)
    def _():
        start_all(b + 1, 1 - slot)

    kibuf[slot, :, past:lp] = kin_ref[0]
    kbuf[slot, :, past:lp] = kn_ref[0]
    vbuf[slot, :, past:lp] = vn_ref[0]

    wait_all(b, slot)

    s_idx = jnp.dot(qi_ref[0].astype(BF16), kibuf[slot].astype(BF16),
                    preferred_element_type=F32)
    wi = wi_ref[0] * (IDX_DIM ** -0.5)
    score = None
    for h in range(IDX_HEADS):
        term = jnp.maximum(s_idx[h * t_new:(h + 1) * t_new, :], 0.0) * wi[:, h:h + 1]
        score = term if score is None else score + term
    q_pos = past + lax.broadcasted_iota(jnp.int32, (t_new, lp), 0)
    k_pos = lax.broadcasted_iota(jnp.int32, (t_new, lp), 1)
    bias = _topk_bias([jnp.where(k_pos <= q_pos, score, NEG_INF)], ksel, *TOPK_SEARCH_WIDE)[0]

    qb = (q_ref[0] * (HEAD_DIM ** -0.5)).astype(BF16)
    s = jnp.dot(qb, kbuf[slot].astype(BF16), preferred_element_type=F32)
    s = s + jnp.concatenate([bias] * ATT_HEADS, axis=0)
    m = jnp.max(s, axis=-1, keepdims=True)
    p = jnp.exp(s - m)
    l = jnp.sum(p, axis=-1, keepdims=True)
    o_ref[0] = lax.dot_general(p.astype(BF16), vbuf[slot].astype(BF16), NT, preferred_element_type=F32) / l


def _dsa_sample(page_table, qi_ht, wi, q_ht, kit_new, kt_new, vt_new, cki_t, ck_t, cv_t):
    B, n_pages = page_table.shape
    page = cki_t.shape[2]
    t_new = wi.shape[1]
    past = n_pages * page
    lp = past + LANES
    ksel = min(TOPK_MAX, (past + t_new) // 4)
    rows = ATT_HEADS * t_new
    kvw = N_KV_HEADS * HEAD_DIM
    grid_spec = pltpu.PrefetchScalarGridSpec(
        num_scalar_prefetch=1,
        grid=(B,),
        in_specs=[
            pl.BlockSpec((1, rows, IDX_DIM), lambda b, pt: (b, 0, 0)),
            pl.BlockSpec((1, t_new, LANES), lambda b, pt: (b, 0, 0)),
            pl.BlockSpec((1, rows, LANES), lambda b, pt: (b, 0, 0)),
            pl.BlockSpec((1, IDX_DIM, LANES), lambda b, pt: (b, 0, 0)),
            pl.BlockSpec((1, kvw, LANES), lambda b, pt: (b, 0, 0)),
            pl.BlockSpec((1, kvw, LANES), lambda b, pt: (b, 0, 0)),
            pl.BlockSpec(memory_space=pl.ANY),
            pl.BlockSpec(memory_space=pl.ANY),
            pl.BlockSpec(memory_space=pl.ANY),
        ],
        out_specs=pl.BlockSpec((1, rows, LANES), lambda b, pt: (b, 0, 0)),
        scratch_shapes=[
            pltpu.VMEM((2, IDX_DIM, lp), F32),
            pltpu.VMEM((2, kvw, lp), F32),
            pltpu.VMEM((2, kvw, lp), F32),
            pltpu.SemaphoreType.DMA((2, 3)),
        ],
    )
    return pl.pallas_call(
        functools.partial(_dsa_sample_body, n_pages=n_pages, page=page, t_new=t_new, ksel=ksel),
        grid_spec=grid_spec,
        out_shape=jax.ShapeDtypeStruct((B, rows, LANES), F32),
        compiler_params=_cparams(("arbitrary",)),
        name="dsa_sample",
    )(page_table, qi_ht, wi, q_ht, kit_new, kt_new, vt_new, cki_t, ck_t, cv_t)


def _gdn_body(qkv_ref, z_ref, sm_ref, gt_ref, cw_ref, alr_ref, dtr_ref, alc_ref, dtc_ref, nw_ref, c0_ref, s0_ref,
              o_ref, sT_ref, cT_ref, xbuf, qs, ks, vs, bcol, gcol, gcc, gcr, us, ws, qes, kds, qks,
              s0_scr, s1_scr, s2_scr, s3_scr, *, ti, tt, chunk):
    i = pl.program_id(1)
    n_i = pl.num_programs(1)
    nq = GDN_HEADS * GDN_DK
    ncb = tt // chunk
    halo = 8
    s_scr = (s0_scr, s1_scr, s2_scr, s3_scr)

    @pl.when(i == 0)
    def _():
        for h in range(GDN_HEADS):
            s_scr[h][...] = s0_ref[0, h]
        xbuf[halo - (CONV_K - 1):halo, :] = c0_ref[0]

    xbuf[halo:halo + ti, :] = qkv_ref[...]
    new_tail = xbuf[halo + ti - (CONV_K - 1):halo + ti, :]
    cw = cw_ref[...]
    y = None
    for j in range(CONV_K):
        term = xbuf[halo - (CONV_K - 1) + j:halo - (CONV_K - 1) + j + ti, :] * cw[j:j + 1, :]
        y = term if y is None else y + term
    y = _silu(y)
    xbuf[halo - (CONV_K - 1):halo, :] = new_tail

    @pl.when(i == n_i - 1)
    def _():
        cT_ref[0] = new_tail

    if ti < tt:
        zpad = jnp.zeros((tt - ti, nq), F32)
        qs[ti:tt, :] = zpad
        ks[ti:tt, :] = zpad
        vs[ti:tt, :] = zpad
        bcol[ti:tt, :] = zpad[:, :LANES]
        gcol[ti:tt, :] = zpad[:, :LANES]
    for h in range(GDN_HEADS):
        qh = y[:, h * GDN_DK:(h + 1) * GDN_DK]
        kh = y[:, nq + h * GDN_DK:nq + (h + 1) * GDN_DK]
        qs[0:ti, h * GDN_DK:(h + 1) * GDN_DK] = (qh * lax.rsqrt(jnp.sum(qh * qh, -1, keepdims=True) + NORM_EPS)
                                                 * (GDN_DK ** -0.5))
        ks[0:ti, h * GDN_DK:(h + 1) * GDN_DK] = kh * lax.rsqrt(jnp.sum(kh * kh, -1, keepdims=True) + NORM_EPS)
    vs[0:ti, :] = y[:, 2 * nq:]
    sm = sm_ref[...]
    bcol[0:ti, :] = jax.nn.sigmoid(sm)
    gcol[0:ti, :] = -jnp.exp(alr_ref[...]) * _softplus(sm + dtr_ref[...])

    ri = lax.broadcasted_iota(jnp.int32, (chunk, chunk), 0)
    ci = lax.broadcasted_iota(jnp.int32, (chunk, chunk), 1)
    causal = ri >= ci
    strict = ri > ci
    ltri = jnp.where(causal, 1.0, 0.0)
    utri = jnp.where(ri <= ci, 1.0, 0.0)
    col_live = lax.broadcasted_iota(jnp.int32, (8, chunk), 1) < ti
    n_levels = int(math.log2(chunk))

    for c in range(ncb):
        gcc[c * chunk:(c + 1) * chunk, :] = _mm(ltri, gcol[c * chunk:(c + 1) * chunk, :], NN, "f32")
        g_rows = -jnp.exp(alc_ref[...]) * _softplus(gt_ref[c] + dtc_ref[...])
        gcr[c] = _mm(jnp.where(col_live, g_rows, 0.0), utri, NN, "f32")

    cpi = GDN_CHUNKS_PER_STEP if ncb % GDN_CHUNKS_PER_STEP == 0 else 1

    def factor_step(c, carry):
        prob = range(cpi * GDN_HEADS)
        hd = [p % GDN_HEADS for p in prob]
        rows = [pl.ds(pl.multiple_of((c * cpi + p // GDN_HEADS) * chunk, chunk), chunk) for p in prob]
        hsl = [slice(h * GDN_DK, (h + 1) * GDN_DK) for h in hd]
        gcum_col = [gcc[rows[j * GDN_HEADS], :] for j in range(cpi)]
        gcum_row = [gcr[c * cpi + j] for j in range(cpi)]
        b_all = [bcol[rows[j * GDN_HEADS], :] for j in range(cpi)]
        kc = [ks[rows[p], hsl[p]] for p in prob]
        qc = [qs[rows[p], hsl[p]] for p in prob]
        beta = [b_all[p // GDN_HEADS][:, 8 + hd[p]:9 + hd[p]] for p in prob]
        gc = [gcum_col[p // GDN_HEADS][:, 12 + hd[p]:13 + hd[p]] for p in prob]
        eg = [jnp.exp(gc[p]) for p in prob]
        decay = [jnp.exp(jnp.where(causal, gc[p] - gcum_row[p // GDN_HEADS][4 + hd[p]:5 + hd[p], :], NEG_INF))
                 for p in prob]
        kb = [kc[p] * beta[p] for p in prob]
        gram = [_mm(jnp.concatenate([kb[p], qc[p]], axis=0), kc[p], NT, GDN_MM_GRAM) for p in prob]
        a = [-jnp.where(strict, gram[p][:chunk] * decay[p], 0.0) for p in prob]
        sol = [jnp.concatenate([vs[rows[p], hsl[p]] * beta[p], kb[p] * eg[p]], axis=1) for p in prob]
        for lvl in range(n_levels):
            mode = GDN_MM_SOLVE if lvl < GDN_SOLVE_FINE_LEVELS else "bf16"
            sol = [sol[p] + _mm(a[p], sol[p], NN, mode) for p in prob]
            if lvl + 1 < n_levels:
                a = [_mm(a[p], a[p], NN, "bf16") for p in prob]
        for p in prob:
            qks[hd[p], rows[p], :] = jnp.where(causal, gram[p][chunk:] * decay[p], 0.0)
            us[rows[p], hsl[p]] = sol[p][:, :GDN_DV]
            ws[rows[p], hsl[p]] = sol[p][:, GDN_DV:]
            qes[rows[p], hsl[p]] = qc[p] * eg[p]
            kds[rows[p], hsl[p]] = kc[p] * jnp.exp(gc[p][chunk - 1:chunk, :] - gc[p])
        return carry

    lax.fori_loop(0, ncb // cpi, factor_step, 0)

    def state_step(c, carry):
        r0 = pl.multiple_of(c * chunk, chunk)
        rows = pl.ds(r0, chunk)
        g_last_all = gcc[pl.ds(r0 + chunk - 1, 1), :]
        heads = range(GDN_HEADS)
        hsl = [slice(h * GDN_DK, (h + 1) * GDN_DK) for h in heads]
        s_prev = [s_scr[h][...] for h in heads]
        v_new = [us[rows, hsl[h]] - _mm(ws[rows, hsl[h]], s_prev[h], NN, GDN_MM_STATE) for h in heads]
        s_add = [_mm(kds[rows, hsl[h]], v_new[h], TN, GDN_MM_STATE) for h in heads]
        for h in heads:
            s_scr[h][...] = s_prev[h] * jnp.exp(g_last_all[:, 12 + h:13 + h]) + s_add[h]
        o = [_mm(qes[rows, hsl[h]], s_prev[h], NN, GDN_MM_STATE) + _mm(qks[h, rows, :], v_new[h], NN, GDN_MM_STATE)
             for h in heads]
        for h in heads:
            og = o[h] * lax.rsqrt(jnp.mean(o[h] * o[h], -1, keepdims=True) + NORM_EPS) * nw_ref[...]
            if ti < tt:
                o_ref[:, hsl[h]] = og[:ti] * _silu(z_ref[:, hsl[h]])
            else:
                o_ref[rows, hsl[h]] = og * _silu(z_ref[rows, hsl[h]])
        return carry

    lax.fori_loop(0, ncb, state_step, 0)

    @pl.when(i == n_i - 1)
    def _():
        for h in range(GDN_HEADS):
            sT_ref[0, h] = s_scr[h][...]


def _gdn(qkv, z, small, gates_t, conv_w, a_log, dt_bias, norm_w, conv0, s0, B, T):
    chunk = GDN_CHUNK
    ti = min(ROW_TILE, T)
    tt = max(ti, chunk)
    n_i = T // ti
    ncb = tt // chunk
    pad = (0, LANES - 16)
    alr = jnp.pad(jnp.concatenate([jnp.zeros((12,), F32), a_log]), pad).reshape(1, LANES)
    dtr = jnp.pad(jnp.concatenate([jnp.zeros((12,), F32), dt_bias]), pad).reshape(1, LANES)
    alc = jnp.concatenate([jnp.zeros((4,), F32), a_log]).reshape(8, 1)
    dtc = jnp.concatenate([jnp.zeros((4,), F32), dt_bias]).reshape(8, 1)
    nw = norm_w.reshape(1, GDN_DV)
    C = GDN_CONV_DIM
    W = GDN_WIDTH

    def const(shape):
        return pl.BlockSpec(shape, lambda b, i: (0,) * len(shape))

    o, s_fin, c_fin = pl.pallas_call(
        functools.partial(_gdn_body, ti=ti, tt=tt, chunk=chunk),
        grid=(B, n_i),
        in_specs=[
            pl.BlockSpec((ti, C), lambda b, i: (b * n_i + i, 0)),
            pl.BlockSpec((ti, W), lambda b, i: (b * n_i + i, 0)),
            pl.BlockSpec((ti, LANES), lambda b, i: (b * n_i + i, 0)),
            pl.BlockSpec((ncb, 8, chunk), lambda b, i: (b * n_i + i, 0, 0)),
            const((CONV_K, C)), const((1, LANES)), const((1, LANES)), const((8, 1)), const((8, 1)),
            const((1, GDN_DV)),
            pl.BlockSpec((1, CONV_K - 1, C), lambda b, i: (b, 0, 0)),
            pl.BlockSpec((1, GDN_HEADS, GDN_DK, GDN_DV), lambda b, i: (b, 0, 0, 0)),
        ],
        out_specs=[
            pl.BlockSpec((ti, W), lambda b, i: (b * n_i + i, 0)),
            pl.BlockSpec((1, GDN_HEADS, GDN_DK, GDN_DV), lambda b, i: (b, 0, 0, 0)),
            pl.BlockSpec((1, CONV_K - 1, C), lambda b, i: (b, 0, 0)),
        ],
        out_shape=[
            jax.ShapeDtypeStruct((B * T, W), F32),
            jax.ShapeDtypeStruct((B, GDN_HEADS, GDN_DK, GDN_DV), F32),
            jax.ShapeDtypeStruct((B, CONV_K - 1, C), F32),
        ],
        scratch_shapes=[
            pltpu.VMEM((ti + 8, C), F32),
            pltpu.VMEM((tt, W), F32), pltpu.VMEM((tt, W), F32), pltpu.VMEM((tt, W), F32),
            pltpu.VMEM((tt, LANES), F32), pltpu.VMEM((tt, LANES), F32),
            pltpu.VMEM((tt, LANES), F32), pltpu.VMEM((ncb, 8, chunk), F32),
            pltpu.VMEM((tt, W), F32), pltpu.VMEM((tt, W), F32),
            pltpu.VMEM((tt, W), F32), pltpu.VMEM((tt, W), F32),
            pltpu.VMEM((GDN_HEADS, tt, chunk), F32),
        ] + [pltpu.VMEM((GDN_DK, GDN_DV), F32)] * GDN_HEADS,
        compiler_params=_cparams(("parallel", "arbitrary")),
        name="gated_delta",
    )(qkv, z, small, gates_t, conv_w, alr, dtr, alc, dtc, nw, conv0, s0)
    return o, s_fin, c_fin


def _gates_transposed(small, B, T):
    chunk = GDN_CHUNK
    g = small[:, IDX_HEADS:IDX_HEADS + 2 * GDN_HEADS].reshape(B, T, 2 * GDN_HEADS)
    tpad = -(-T // chunk) * chunk
    g = jnp.pad(g, ((0, 0), (0, tpad - T), (0, 0)))
    return g.reshape(B * tpad // chunk, chunk, 2 * GDN_HEADS).transpose(0, 2, 1)


def _rope_tables(pos):
    d_rot = HEAD_DIM // 4
    half = d_rot // 2
    inv_freq = ROPE_THETA ** (-jnp.arange(half, dtype=F32) / half)
    ang = pos.astype(F32)[:, None] * inv_freq[None, :]
    cos = jnp.cos(ang)
    sin = jnp.sin(ang)
    n = pos.shape[0]
    ones = jnp.ones((n, HEAD_DIM - d_rot), F32)
    zeros = jnp.zeros((n, HEAD_DIM - d_rot), F32)
    zh = jnp.zeros((n, half), F32)
    c = jnp.concatenate([cos, cos, ones], axis=1)
    sa = jnp.concatenate([-sin, zh, zeros], axis=1)
    sb = jnp.concatenate([zh, sin, zeros], axis=1)
    rep = LANES // HEAD_DIM
    return jnp.tile(c, (1, rep)), jnp.tile(sa, (1, rep)), jnp.tile(sb, (1, rep))


def _split_w_in(w_in):
    pts = [0]
    for s in SPLITS:
        pts.append(pts[-1] + s)
    wq, wk, wv, wqi, wki, wwi, wqkv, wz, wb, wa = [w_in[:, pts[n]:pts[n + 1]] for n in range(len(SPLITS))]
    D = w_in.shape[0]
    wr = jnp.concatenate([wq, wqi, wk, wki, jnp.zeros((D, ROPE_W - 2 * ATT_WIDTH - LANES - IDX_DIM), F32)], axis=1)
    ws = jnp.concatenate([wwi, wb, wa, jnp.zeros((D, LANES - IDX_HEADS - 2 * GDN_HEADS), F32)], axis=1)
    return tuple(a.astype(BF16) for a in (wr, wv, ws, wqkv, wz))


def kernel(x_prompt, x_sample, cache_k, cache_v, cache_kidx, state_gdn, state_conv, page_table, ffn1_w_gate, ffn1_w_up, ffn1_w_down, ln1_g, ln1_b, w_in, conv_w, a_log, dt_bias, gdn_norm_w, w_o, ln2_g, ln2_b, ffn2_w_gate, ffn2_w_up, ffn2_w_down, ln3_g, ln3_b):
    B, S, D = x_prompt.shape
    Bs, T, _ = x_sample.shape
    n_pages = page_table.shape[1]
    page = cache_k.shape[2]
    past = n_pages * page
    kvw = N_KV_HEADS * HEAD_DIM
    l = 0
    xp = x_prompt.reshape(B * S, D)
    xs = x_sample.reshape(Bs * T, D)

    f1 = (ffn1_w_gate[l].astype(BF16), ffn1_w_up[l].astype(BF16), ffn1_w_down[l].astype(BF16), ln1_g, ln1_b)
    f2 = (ffn2_w_gate[l].astype(BF16), ffn2_w_up[l].astype(BF16), ffn2_w_down[l].astype(BF16), ln3_g, ln3_b)
    w_proj = _split_w_in(w_in[l])
    wo_att = w_o[l][:ATT_WIDTH].astype(BF16)
    wo_gdn = w_o[l][ATT_WIDTH:].astype(BF16)

    xp1 = _ffn_half_step(xp, *f1)
    tm_p = min(ROW_TILE, B * S)
    q, qi, kt, kit, vt, small, qkv, z, kt_bf, kit_bf, vt_bf = _mix_projection(
        xp1, w_proj, _rope_tables(jnp.arange(S)), S // tm_p, seq=S)
    att = _dsa_prompt(q, qi, small, kt_bf, kit_bf, vt_bf)
    gdn, s_p, c_p = _gdn(qkv, z, small, _gates_transposed(small, B, S), conv_w[l], a_log[l], dt_bias[l],
                         gdn_norm_w[l], jnp.zeros((B, CONV_K - 1, GDN_CONV_DIM), F32),
                         jnp.zeros((B, GDN_HEADS, GDN_DK, GDN_DV), F32), B, S)
    y_prompt = _ffn_half_step(xp1, *f2, mix=(att, gdn, wo_att, wo_gdn, ln2_g, ln2_b)).reshape(B, S, D)
    k_prompt = kt.reshape(1, B, N_KV_HEADS, HEAD_DIM, S).transpose(0, 1, 4, 2, 3)
    v_prompt = vt.reshape(1, B, N_KV_HEADS, HEAD_DIM, S).transpose(0, 1, 4, 2, 3)
    kidx_prompt = kit.reshape(1, B, IDX_DIM, S).transpose(0, 1, 3, 2)

    xs1 = _ffn_half_step(xs, *f1)
    pos_s = jnp.tile(past + jnp.arange(T), Bs)
    tm_s = min(ROW_TILE, Bs * T)
    q, qi, k, ki, v, small, qkv, z = _mix_projection(xs1, w_proj, _rope_tables(pos_s), Bs * T // tm_s)
    group = ATT_HEADS // N_KV_HEADS
    qh = q.reshape(Bs, T, N_KV_HEADS, group, HEAD_DIM).transpose(0, 2, 3, 1, 4)
    zq = jnp.zeros_like(qh[:, 0])
    q_ht = jnp.concatenate([jnp.concatenate([qh[:, 0], zq], -1), jnp.concatenate([zq, qh[:, 1]], -1)], axis=1)
    q_ht = q_ht.reshape(Bs, ATT_HEADS * T, LANES)
    qi_ht = qi.reshape(Bs, T, IDX_HEADS, IDX_DIM).transpose(0, 2, 1, 3).reshape(Bs, IDX_HEADS * T, IDX_DIM)

    def new_tile(a):
        a = a.reshape(Bs, T, a.shape[-1]).transpose(0, 2, 1)
        return jnp.pad(a, ((0, 0), (0, 0), (0, LANES - T)))

    cki_t = cache_kidx[l].transpose(0, 2, 1)
    ck_t = cache_k[l].transpose(0, 2, 3, 1).reshape(-1, kvw, page)
    cv_t = cache_v[l].transpose(0, 2, 3, 1).reshape(-1, kvw, page)
    o_ht = _dsa_sample(page_table, qi_ht, small.reshape(Bs, T, LANES), q_ht, new_tile(ki), new_tile(k), new_tile(v),
                       cki_t, ck_t, cv_t)
    o_ht = o_ht.reshape(Bs, N_KV_HEADS, group, T, N_KV_HEADS, HEAD_DIM)
    att = jnp.stack([o_ht[:, 0, :, :, 0], o_ht[:, 1, :, :, 1]], axis=1)
    att = att.transpose(0, 3, 1, 2, 4).reshape(Bs * T, ATT_WIDTH)
    gdn, s_s, c_s = _gdn(qkv, z, small, _gates_transposed(small, Bs, T), conv_w[l], a_log[l], dt_bias[l],
                         gdn_norm_w[l], state_conv[l], state_gdn[l], Bs, T)
    y_sample = _ffn_half_step(xs1, *f2, mix=(att, gdn, wo_att, wo_gdn, ln2_g, ln2_b)).reshape(Bs, T, D)

    return (y_prompt, y_sample,
            k_prompt, v_prompt, kidx_prompt, s_p[None], c_p[None],
            k.reshape(1, Bs, T, N_KV_HEADS, HEAD_DIM), v.reshape(1, Bs, T, N_KV_HEADS, HEAD_DIM),
            ki.reshape(1, Bs, T, IDX_DIM), s_s[None], c_s[None])
```

```python
import functools
import math

import jax
import jax.numpy as jnp
from jax import lax
from jax.experimental import pallas as pl
from jax.experimental.pallas import tpu as pltpu

F32 = jnp.float32
BF16 = jnp.bfloat16
HI = lax.Precision.HIGHEST

LANES = 128
VMEM_LIMIT = 56 * 1024 * 1024

D_MODEL = 1024
DEPTH = 1
ATT_HEADS = 8
N_KV_HEADS = 2
HEAD_DIM = 64
ATT_WIDTH = ATT_HEADS * HEAD_DIM
ROPE_THETA = 500000.0
IDX_HEADS = 8
IDX_DIM = 64
TOPK_MAX = 256
GDN_HEADS = 4
GDN_DK = 128
GDN_DV = 128
GDN_WIDTH = GDN_HEADS * GDN_DV
GDN_CONV_DIM = 2 * GDN_HEADS * GDN_DK + GDN_WIDTH
CONV_K = 4
GDN_CHUNK = 64
LN_EPS = 1e-5
NORM_EPS = 1e-6
DEEP_ALPHA = (2 * DEPTH) ** 0.25
SPLITS = (ATT_WIDTH, N_KV_HEADS * HEAD_DIM, N_KV_HEADS * HEAD_DIM, IDX_HEADS * IDX_DIM, IDX_DIM, IDX_HEADS,
          GDN_CONV_DIM, GDN_WIDTH, GDN_HEADS, GDN_HEADS)
ROPE_W = 1280
TOPK_SEARCH_BINARY = (20, 2)
TOPK_SEARCH_WIDE = (10, 4)
TOPK_WIDE_MAX_ELEMS = 128 * 1024

NEG_INF = float("-inf")
POS_INF = float("inf")
NN = (((1,), (0,)), ((), ()))
NT = (((1,), (1,)), ((), ()))
TN = (((0,), (0,)), ((), ()))
ROW_TILE = 512
QUERY_TILE = 128
IDX_KEY_CHUNK = 256
CAUSAL_CLASSES = 4
GDN_MM_GRAM = "bf16"
GDN_MM_SOLVE = "x3"
GDN_SOLVE_FINE_LEVELS = 2
GDN_CHUNKS_PER_STEP = 4
GDN_MM_STATE = "bf16"


def _cparams(sem):
    return pltpu.CompilerParams(dimension_semantics=sem, vmem_limit_bytes=VMEM_LIMIT)


def _layer_norm_rows(y, g, b):
    mu = jnp.mean(y, axis=-1, keepdims=True)
    d = y - mu
    var = jnp.mean(d * d, axis=-1, keepdims=True)
    return d * lax.rsqrt(var + LN_EPS) * g + b


def _silu(x):
    return x * jax.nn.sigmoid(x)


def _softplus(x):
    return jnp.maximum(x, 0.0) + jnp.log(1.0 + jnp.exp(-jnp.abs(x)))


def _split_bf16(x):
    hi = x.astype(BF16)
    return hi, (x - hi.astype(F32)).astype(BF16)


def _mm(a, b, dims, mode):
    if mode == "f32":
        return lax.dot_general(a, b, dims, precision=HI, preferred_element_type=F32)
    dot = functools.partial(lax.dot_general, dimension_numbers=dims, preferred_element_type=F32)
    if mode == "bf16":
        return dot(a.astype(BF16), b.astype(BF16))
    ah, al = _split_bf16(a)
    bh, bl = _split_bf16(b)
    return dot(ah, bh) + (dot(ah, bl) + dot(al, bh))


def _ffn_body(*refs, mixed):
    j = pl.program_id(1)
    if mixed:
        (x_in_ref, att_ref, gdn_ref, woa_ref, wog_ref, g2_ref, b2_ref,
         wg_ref, wu_ref, wd_ref, g_ref, b_ref, o_ref, acc_ref, x_ref) = refs

        @pl.when(j == 0)
        def _():
            mix = (jnp.dot(att_ref[...].astype(BF16), woa_ref[...], preferred_element_type=F32)
                   + jnp.dot(gdn_ref[...].astype(BF16), wog_ref[...], preferred_element_type=F32))
            x_ref[...] = _layer_norm_rows(DEEP_ALPHA * x_in_ref[...] + mix, g2_ref[...], b2_ref[...])
    else:
        x_ref, wg_ref, wu_ref, wd_ref, g_ref, b_ref, o_ref, acc_ref = refs
    xb = x_ref[...].astype(BF16)
    hg = jnp.dot(xb, wg_ref[...], preferred_element_type=F32)
    hu = jnp.dot(xb, wu_ref[...], preferred_element_type=F32)
    h = _silu(hg) * hu
    part = jnp.dot(h.astype(BF16), wd_ref[...], preferred_element_type=F32)

    @pl.when(j == 0)
    def _():
        acc_ref[...] = part

    @pl.when(j > 0)
    def _():
        acc_ref[...] += part

    @pl.when(j == pl.num_programs(1) - 1)
    def _():
        y = DEEP_ALPHA * x_ref[...] + 0.5 * acc_ref[...]
        o_ref[...] = _layer_norm_rows(y, g_ref[...], b_ref[...])


def _ffn_half_step(x, wg, wu, wd, g, b, mix=None):
    M, D = x.shape
    FF = wg.shape[1]
    tm = min(ROW_TILE, M)
    tf = FF // 2 if (FF // 2) % LANES == 0 else FF

    def rows(width):
        return pl.BlockSpec((tm, width), lambda i, j: (i, 0))

    def full(a):
        return pl.BlockSpec(a.shape, lambda i, j: (0, 0))

    ffn_specs = [
        pl.BlockSpec((D, tf), lambda i, j: (0, j)),
        pl.BlockSpec((D, tf), lambda i, j: (0, j)),
        pl.BlockSpec((tf, D), lambda i, j: (j, 0)),
        full(g), full(b),
    ]
    scratch = [pltpu.VMEM((tm, D), F32)]
    if mix is None:
        operands = (x, wg, wu, wd, g, b)
        in_specs = [rows(D)] + ffn_specs
    else:
        att, gdn, wo_att, wo_gdn, g2, b2 = mix
        operands = (x, att, gdn, wo_att, wo_gdn, g2, b2, wg, wu, wd, g, b)
        in_specs = [rows(D), rows(att.shape[1]), rows(gdn.shape[1]), full(wo_att), full(wo_gdn), full(g2),
                    full(b2)] + ffn_specs
        scratch.append(pltpu.VMEM((tm, D), F32))
    return pl.pallas_call(
        functools.partial(_ffn_body, mixed=mix is not None),
        grid=(M // tm, FF // tf),
        in_specs=in_specs,
        out_specs=rows(D),
        out_shape=jax.ShapeDtypeStruct((M, D), F32),
        scratch_shapes=scratch,
        compiler_params=_cparams(("parallel", "arbitrary")),
        name="ffn_half_step",
    )(*operands)


def _proj_body(x_ref, wr_ref, wv_ref, ws_ref, wqkv_ref, wz_ref, cos_ref, sa_ref, sb_ref,
               q_ref, qi_ref, k_ref, ki_ref, v_ref, small_ref, qkv_ref, z_ref, *bf16_refs, transposed):
    xb = x_ref[...].astype(BF16)
    r = jnp.dot(xb, wr_ref[...], preferred_element_type=F32)
    c = cos_ref[...]
    sa = sa_ref[...]
    sb = sb_ref[...]

    def rope(slab):
        return slab * c + pltpu.roll(slab, LANES - 8, 1) * sa + pltpu.roll(slab, 8, 1) * sb

    for s in range(ATT_WIDTH // LANES):
        q_ref[:, s * LANES:(s + 1) * LANES] = rope(r[:, s * LANES:(s + 1) * LANES])
    off = ATT_WIDTH
    for s in range(IDX_HEADS * IDX_DIM // LANES):
        qi_ref[:, s * LANES:(s + 1) * LANES] = rope(r[:, off + s * LANES:off + (s + 1) * LANES])
    off += IDX_HEADS * IDX_DIM
    kr = rope(r[:, off:off + LANES])
    kir = rope(r[:, off + LANES:off + 2 * LANES])
    v = jnp.dot(xb, wv_ref[...], preferred_element_type=F32)
    if transposed:
        kb_ref, kib_ref, vb_ref = bf16_refs
        kt = kr.T
        kit = kir.T[:IDX_DIM, :]
        vt = v.T
        k_ref[0] = kt
        ki_ref[0] = kit
        v_ref[0] = vt
        kb_ref[0] = kt.astype(BF16)
        kib_ref[0] = kit.astype(BF16)
        vb_ref[0] = vt.astype(BF16)
    else:
        k_ref[...] = kr
        ki_ref[...] = kir[:, :IDX_DIM]
        v_ref[...] = v
    sm = jnp.dot(xb, ws_ref[...], preferred_element_type=F32)
    lane = lax.broadcasted_iota(jnp.int32, sm.shape, 1)
    small_ref[...] = jnp.where(lane < IDX_HEADS, sm * (IDX_HEADS ** -0.5), sm)
    qkv_ref[...] = jnp.dot(xb, wqkv_ref[...], preferred_element_type=F32)
    z_ref[...] = jnp.dot(xb, wz_ref[...], preferred_element_type=F32)


def _mix_projection(x, w, tabs, n_tab_blocks, seq=None):
    M, D = x.shape
    tm = min(ROW_TILE, M)
    wr, wv, ws, wqkv, wz = w
    cos_t, sa_t, sb_t = tabs
    kvw = N_KV_HEADS * HEAD_DIM

    def full(a):
        return pl.BlockSpec(a.shape, lambda i: (0, 0))

    def rows(width):
        return pl.BlockSpec((tm, width), lambda i: (i, 0))

    tab_spec = pl.BlockSpec((tm, LANES), lambda i: (i % n_tab_blocks, 0))
    if seq is None:
        kv_shapes = [jax.ShapeDtypeStruct((M, kvw), F32), jax.ShapeDtypeStruct((M, IDX_DIM), F32),
                     jax.ShapeDtypeStruct((M, kvw), F32)]
        kv_specs = [rows(kvw), rows(IDX_DIM), rows(kvw)]
    else:
        nt = seq // tm
        nb = M // seq

        def tr(width):
            return pl.BlockSpec((1, width, tm), lambda i: (i // nt, 0, i % nt))

        kv_shapes = [jax.ShapeDtypeStruct((nb, kvw, seq), F32), jax.ShapeDtypeStruct((nb, IDX_DIM, seq), F32),
                     jax.ShapeDtypeStruct((nb, kvw, seq), F32)]
        kv_specs = [tr(kvw), tr(IDX_DIM), tr(kvw)]
    out_shapes = [
        jax.ShapeDtypeStruct((M, ATT_WIDTH), F32),
        jax.ShapeDtypeStruct((M, IDX_HEADS * IDX_DIM), F32),
    ] + kv_shapes + [
        jax.ShapeDtypeStruct((M, LANES), F32),
        jax.ShapeDtypeStruct((M, GDN_CONV_DIM), F32),
        jax.ShapeDtypeStruct((M, GDN_WIDTH), F32),
    ]
    out_specs = [rows(ATT_WIDTH), rows(IDX_HEADS * IDX_DIM)] + kv_specs + [
        rows(LANES), rows(GDN_CONV_DIM), rows(GDN_WIDTH)]
    if seq is not None:
        out_shapes += [jax.ShapeDtypeStruct(s.shape, BF16) for s in kv_shapes]
        out_specs += kv_specs
    return pl.pallas_call(
        functools.partial(_proj_body, transposed=seq is not None),
        grid=(M // tm,),
        in_specs=[rows(D), full(wr), full(wv), full(ws), full(wqkv), full(wz), tab_spec, tab_spec, tab_spec],
        out_specs=out_specs,
        out_shape=out_shapes,
        compiler_params=_cparams(("parallel",)),
        name="mix_projection",
    )(x, wr, wv, ws, wqkv, wz, cos_t, sa_t, sb_t)


def _count_ge(xm, thr):
    return jnp.sum(jnp.where(xm >= thr, 1.0, 0.0), axis=-1, keepdims=True)


def _prefix_count(eq_f32):
    R, L = eq_f32.shape
    ri = lax.broadcasted_iota(jnp.int32, (LANES, LANES), 0)
    ci = lax.broadcasted_iota(jnp.int32, (LANES, LANES), 1)
    tri = jnp.where(ri <= ci, 1.0, 0.0).astype(BF16)
    offset = jnp.zeros((R, 1), F32)
    pieces = []
    for blk in range(L // LANES):
        e = eq_f32[:, blk * LANES:(blk + 1) * LANES].astype(BF16)
        loc = jnp.dot(e, tri, preferred_element_type=F32)
        pieces.append(loc + offset)
        offset = offset + loc[:, LANES - 1:LANES]
    return jnp.concatenate(pieces, axis=1)


def _topk_bias(xms, k, n_steps, arity, bounds=None):
    kf = float(k)
    n = len(xms)
    grp = range(n)
    if bounds is None:
        rowmax = [jnp.max(xm, axis=-1, keepdims=True) for xm in xms]
        lo0 = [jnp.min(jnp.where(xm == NEG_INF, POS_INF, xm), axis=-1, keepdims=True) for xm in xms]
    else:
        lo0 = [b[0] for b in bounds]
        rowmax = [b[1] for b in bounds]
    hi0 = [m + jnp.abs(m) * (2.0 ** -10) + 1.0 for m in rowmax]

    def search(_, c):
        lo, hi = c[:n], c[n:]
        mids = [[lo[g] + (hi[g] - lo[g]) * (j / arity) for j in range(1, arity)] for g in grp]
        ge = [[_count_ge(xms[g], m) >= kf for m in mids[g]] for g in grp]
        new_lo, new_hi = [], []
        for g in grp:
            nlo, nhi = lo[g], hi[g]
            for j in range(arity - 1):
                nlo = jnp.where(ge[g][j], mids[g][j], nlo)
            for j in reversed(range(arity - 1)):
                nhi = jnp.where(ge[g][j], nhi, mids[g][j])
            new_lo.append(nlo)
            new_hi.append(nhi)
        return tuple(new_lo) + tuple(new_hi)

    c = lax.fori_loop(0, n_steps, search, tuple(lo0) + tuple(hi0))
    lo, hi = list(c[:n]), list(c[n:])
    c_lo = [_count_ge(xms[g], lo[g]) for g in grp]
    pending = [jnp.where(c_lo[g] > kf, 1.0, 0.0) for g in grp]

    def cond(c):
        worst = jnp.max(c[3 * n])
        for g in range(1, n):
            worst = jnp.maximum(worst, jnp.max(c[3 * n + g]))
        return worst > 0.0

    def body(c):
        lo, hi, c_lo, pending = c[:n], c[n:2 * n], c[2 * n:3 * n], c[3 * n:]
        t = [jnp.max(jnp.where(xms[g] < hi[g], xms[g], NEG_INF), axis=-1, keepdims=True) for g in grp]
        ct = [_count_ge(xms[g], t[g]) for g in grp]
        out = [[], [], [], []]
        for g in grp:
            hit = ct[g] >= kf
            live = pending[g] > 0.0
            upd = jnp.logical_and(live, hit)
            out[0].append(jnp.where(upd, t[g], lo[g]))
            out[1].append(jnp.where(jnp.logical_and(live, jnp.logical_not(hit)), t[g], hi[g]))
            out[2].append(jnp.where(upd, ct[g], c_lo[g]))
            out[3].append(jnp.where(hit, 0.0, pending[g]))
        return tuple(out[0]) + tuple(out[1]) + tuple(out[2]) + tuple(out[3])

    c = lax.while_loop(cond, body, tuple(lo) + tuple(hi) + tuple(c_lo) + tuple(pending))
    lo, c_lo = c[:n], c[2 * n:3 * n]

    def select(xm, lo, c_lo):
        def plain():
            return jnp.where(xm >= lo, 0.0, NEG_INF)

        def with_ties():
            gt = xm > lo
            eq = xm == lo
            neg_zero = jnp.logical_and(xm == 0.0, 1.0 / xm < 0.0)
            eq_hi = jnp.logical_and(eq, jnp.logical_not(neg_zero))
            eq_lo = jnp.logical_and(eq, neg_zero)
            room = kf - jnp.sum(jnp.where(gt, 1.0, 0.0), axis=-1, keepdims=True)
            rank_hi = _prefix_count(jnp.where(eq_hi, 1.0, 0.0))
            rank_lo = _prefix_count(jnp.where(eq_lo, 1.0, 0.0)) + rank_hi[:, -1:]
            keep = jnp.logical_or(gt, jnp.logical_or(jnp.logical_and(eq_hi, rank_hi <= room),
                                                     jnp.logical_and(eq_lo, rank_lo <= room)))
            return jnp.where(keep, 0.0, NEG_INF)

        return lax.cond(jnp.max(c_lo) > kf, with_ties, plain)

    return [select(xms[g], lo[g], c_lo[g]) for g in grp]


def _indexer_scores(qi_bf, kit_ref, wi, sk):
    ws = wi * (IDX_DIM ** -0.5)
    pieces = []
    lo_run = hi_run = None
    for kc in range(sk // IDX_KEY_CHUNK):
        kit = kit_ref[0, :, kc * IDX_KEY_CHUNK:(kc + 1) * IDX_KEY_CHUNK]
        acc = None
        for h in range(IDX_HEADS):
            s = jnp.dot(qi_bf[:, h * IDX_DIM:(h + 1) * IDX_DIM], kit, preferred_element_type=F32)
            term = jnp.maximum(s, 0.0) * ws[:, h:h + 1]
            acc = term if acc is None else acc + term
        pieces.append(acc)
        for j in range(IDX_KEY_CHUNK // LANES):
            part = acc[:, j * LANES:(j + 1) * LANES]
            lo_run = part if lo_run is None else jnp.minimum(lo_run, part)
            hi_run = part if hi_run is None else jnp.maximum(hi_run, part)
    return (jnp.concatenate(pieces, axis=1), jnp.min(lo_run, axis=-1, keepdims=True),
            jnp.max(hi_run, axis=-1, keepdims=True))


def _dsa_prompt_block(q_ref, qi_ref, sm_ref, kt_ref, kit_ref, vt_ref, o_ref, i, *, tq, ksel, sk, dense):
    q_pos = i * tq + lax.broadcasted_iota(jnp.int32, (tq, sk), 0)
    k_pos = lax.broadcasted_iota(jnp.int32, (tq, sk), 1)
    if dense:
        bias = jnp.where(k_pos <= q_pos, 0.0, NEG_INF)
    else:
        score, smin, smax = _indexer_scores(qi_ref[...].astype(BF16), kit_ref, sm_ref[...], sk)
        xm = jnp.where(k_pos <= q_pos, score, NEG_INF)
        search = TOPK_SEARCH_WIDE if tq * sk <= TOPK_WIDE_MAX_ELEMS else TOPK_SEARCH_BINARY
        bias = _topk_bias([xm], ksel, *search, bounds=[(smin, smax)])[0]

    kt = kt_ref[0, :, :sk]
    vt = vt_ref[0, :, :sk]
    qb = (q_ref[...] * (HEAD_DIM ** -0.5)).astype(BF16)
    group = ATT_HEADS // N_KV_HEADS
    lane = lax.broadcasted_iota(jnp.int32, (tq, LANES), 1)
    outs = []
    for h in range(ATT_HEADS):
        kv = h // group
        s = jnp.dot(qb[:, h * HEAD_DIM:(h + 1) * HEAD_DIM], kt[kv * HEAD_DIM:(kv + 1) * HEAD_DIM, :],
                    preferred_element_type=F32) + bias
        m = jnp.max(s, axis=-1, keepdims=True)
        p = jnp.exp(s - m)
        l = jnp.sum(p, axis=-1, keepdims=True)
        outs.append(lax.dot_general(p.astype(BF16), vt, NT, preferred_element_type=F32) / l)
    for pair in range(ATT_HEADS // 2):
        h0, h1 = 2 * pair, 2 * pair + 1
        kv = h0 // group
        a, b = outs[h0], outs[h1]
        if kv == 0:
            b = pltpu.roll(b, HEAD_DIM, 1)
        else:
            a = pltpu.roll(a, HEAD_DIM, 1)
        o_ref[:, pair * LANES:(pair + 1) * LANES] = jnp.where(lane < HEAD_DIM, a, b)


def _dsa_prompt_body(q_ref, qi_ref, sm_ref, kt_ref, kit_ref, vt_ref, o_ref, *, tq, ksel, n_classes):
    i = pl.program_id(1)
    per = kt_ref.shape[2] // tq // n_classes
    n_dense = min(ksel // tq, per)
    if n_dense:
        @pl.when(i < n_dense)
        def _():
            _dsa_prompt_block(q_ref, qi_ref, sm_ref, kt_ref, kit_ref, vt_ref, o_ref, i,
                              tq=tq, ksel=ksel, sk=n_dense * tq, dense=True)
    for c in range(n_classes):
        @pl.when(jnp.logical_and(i >= max(c * per, n_dense), i < (c + 1) * per))
        def _(c=c):
            _dsa_prompt_block(q_ref, qi_ref, sm_ref, kt_ref, kit_ref, vt_ref, o_ref, i,
                              tq=tq, ksel=ksel, sk=(c + 1) * per * tq, dense=False)


def _dsa_prompt(q, qi, small, kt, kit, vt):
    B, kvw, S = kt.shape
    tq = QUERY_TILE
    nq = S // tq
    ksel = min(TOPK_MAX, S // 4)
    n_classes = CAUSAL_CLASSES if nq % CAUSAL_CLASSES == 0 and (nq // CAUSAL_CLASSES * tq) % IDX_KEY_CHUNK == 0 else 1
    return pl.pallas_call(
        functools.partial(_dsa_prompt_body, tq=tq, ksel=ksel, n_classes=n_classes),
        grid=(B, nq),
        in_specs=[
            pl.BlockSpec((tq, ATT_WIDTH), lambda b, i: (b * nq + i, 0)),
            pl.BlockSpec((tq, IDX_HEADS * IDX_DIM), lambda b, i: (b * nq + i, 0)),
            pl.BlockSpec((tq, LANES), lambda b, i: (b * nq + i, 0)),
            pl.BlockSpec((1, kvw, S), lambda b, i: (b, 0, 0)),
            pl.BlockSpec((1, IDX_DIM, S), lambda b, i: (b, 0, 0)),
            pl.BlockSpec((1, kvw, S), lambda b, i: (b, 0, 0)),
        ],
        out_specs=pl.BlockSpec((tq, ATT_WIDTH), lambda b, i: (b * nq + i, 0)),
        out_shape=jax.ShapeDtypeStruct((B * S, ATT_WIDTH), F32),
        compiler_params=_cparams(("parallel", "arbitrary")),
        name="dsa_prompt",
    )(q, qi, small, kt, kit, vt)


def _dsa_sample_body(pt_ref, qi_ref, wi_ref, q_ref, kin_ref, kn_ref, vn_ref, cki_hbm, ck_hbm, cv_hbm, o_ref,
                     kibuf, kbuf, vbuf, sem, *, n_pages, page, t_new, ksel):
    b = pl.program_id(0)
    nb = pl.num_programs(0)
    slot = lax.rem(b, 2)
    past = n_pages * page
    lp = kibuf.shape[2]

    def page_copies(bb, sl, p):
        pg = pt_ref[bb, p]
        cols = pl.ds(p * page, page)
        return (pltpu.make_async_copy(cki_hbm.at[pg], kibuf.at[sl, :, cols], sem.at[sl, 0]),
                pltpu.make_async_copy(ck_hbm.at[pg], kbuf.at[sl, :, cols], sem.at[sl, 1]),
                pltpu.make_async_copy(cv_hbm.at[pg], vbuf.at[sl, :, cols], sem.at[sl, 2]))

    def start_all(bb, sl):
        for p in range(n_pages):
            for cp in page_copies(bb, sl, p):
                cp.start()

    def wait_all(bb, sl):
        for p in range(n_pages):
            for cp in page_copies(bb, sl, p):
                cp.wait()

    @pl.when(b == 0)
    def _():
        start_all(0, 0)

    @pl.when(b + 1 < nb)
    def _():
        start_all(b + 1, 1 - slot)

    kibuf[slot, :, past:lp] = kin_ref[0]
    kbuf[slot, :, past:lp] = kn_ref[0]
    vbuf[slot, :, past:lp] = vn_ref[0]

    wait_all(b, slot)

    s_idx = jnp.dot(qi_ref[0].astype(BF16), kibuf[slot].astype(BF16),
                    preferred_element_type=F32)
    wi = wi_ref[0] * (IDX_DIM ** -0.5)
    score = None
    for h in range(IDX_HEADS):
        term = jnp.maximum(s_idx[h * t_new:(h + 1) * t_new, :], 0.0) * wi[:, h:h + 1]
        score = term if score is None else score + term
    q_pos = past + lax.broadcasted_iota(jnp.int32, (t_new, lp), 0)
    k_pos = lax.broadcasted_iota(jnp.int32, (t_new, lp), 1)
    bias = _topk_bias([jnp.where(k_pos <= q_pos, score, NEG_INF)], ksel, *TOPK_SEARCH_WIDE)[0]

    qb = (q_ref[0] * (HEAD_DIM ** -0.5)).astype(BF16)
    s = jnp.dot(qb, kbuf[slot].astype(BF16), preferred_element_type=F32)
    s = s + jnp.concatenate([bias] * ATT_HEADS, axis=0)
    m = jnp.max(s, axis=-1, keepdims=True)
    p = jnp.exp(s - m)
    l = jnp.sum(p, axis=-1, keepdims=True)
    o_ref[0] = lax.dot_general(p.astype(BF16), vbuf[slot].astype(BF16), NT, preferred_element_type=F32) / l


def _dsa_sample(page_table, qi_ht, wi, q_ht, kit_new, kt_new, vt_new, cki_t, ck_t, cv_t):
    B, n_pages = page_table.shape
    page = cki_t.shape[2]
    t_new = wi.shape[1]
    past = n_pages * page
    lp = past + LANES
    ksel = min(TOPK_MAX, (past + t_new) // 4)
    rows = ATT_HEADS * t_new
    kvw = N_KV_HEADS * HEAD_DIM
    grid_spec = pltpu.PrefetchScalarGridSpec(
        num_scalar_prefetch=1,
        grid=(B,),
        in_specs=[
            pl.BlockSpec((1, rows, IDX_DIM), lambda b, pt: (b, 0, 0)),
            pl.BlockSpec((1, t_new, LANES), lambda b, pt: (b, 0, 0)),
            pl.BlockSpec((1, rows, LANES), lambda b, pt: (b, 0, 0)),
            pl.BlockSpec((1, IDX_DIM, LANES), lambda b, pt: (b, 0, 0)),
            pl.BlockSpec((1, kvw, LANES), lambda b, pt: (b, 0, 0)),
            pl.BlockSpec((1, kvw, LANES), lambda b, pt: (b, 0, 0)),
            pl.BlockSpec(memory_space=pl.ANY),
            pl.BlockSpec(memory_space=pl.ANY),
            pl.BlockSpec(memory_space=pl.ANY),
        ],
        out_specs=pl.BlockSpec((1, rows, LANES), lambda b, pt: (b, 0, 0)),
        scratch_shapes=[
            pltpu.VMEM((2, IDX_DIM, lp), F32),
            pltpu.VMEM((2, kvw, lp), F32),
            pltpu.VMEM((2, kvw, lp), F32),
            pltpu.SemaphoreType.DMA((2, 3)),
        ],
    )
    return pl.pallas_call(
        functools.partial(_dsa_sample_body, n_pages=n_pages, page=page, t_new=t_new, ksel=ksel),
        grid_spec=grid_spec,
        out_shape=jax.ShapeDtypeStruct((B, rows, LANES), F32),
        compiler_params=_cparams(("arbitrary",)),
        name="dsa_sample",
    )(page_table, qi_ht, wi, q_ht, kit_new, kt_new, vt_new, cki_t, ck_t, cv_t)


def _gdn_body(qkv_ref, z_ref, sm_ref, gt_ref, cw_ref, alr_ref, dtr_ref, alc_ref, dtc_ref, nw_ref, c0_ref, s0_ref,
              o_ref, sT_ref, cT_ref, xbuf, qs, ks, vs, bcol, gcol, gcc, gcr, us, ws, qes, kds, qks,
              s0_scr, s1_scr, s2_scr, s3_scr, *, ti, tt, chunk):
    i = pl.program_id(1)
    n_i = pl.num_programs(1)
    nq = GDN_HEADS * GDN_DK
    ncb = tt // chunk
    halo = 8
    s_scr = (s0_scr, s1_scr, s2_scr, s3_scr)

    @pl.when(i == 0)
    def _():
        for h in range(GDN_HEADS):
            s_scr[h][...] = s0_ref[0, h]
        xbuf[halo - (CONV_K - 1):halo, :] = c0_ref[0]

    xbuf[halo:halo + ti, :] = qkv_ref[...]
    new_tail = xbuf[halo + ti - (CONV_K - 1):halo + ti, :]
    cw = cw_ref[...]
    y = None
    for j in range(CONV_K):
        term = xbuf[halo - (CONV_K - 1) + j:halo - (CONV_K - 1) + j + ti, :] * cw[j:j + 1, :]
        y = term if y is None else y + term
    y = _silu(y)
    xbuf[halo - (CONV_K - 1):halo, :] = new_tail

    @pl.when(i == n_i - 1)
    def _():
        cT_ref[0] = new_tail

    if ti < tt:
        zpad = jnp.zeros((tt - ti, nq), F32)
        qs[ti:tt, :] = zpad
        ks[ti:tt, :] = zpad
        vs[ti:tt, :] = zpad
        bcol[ti:tt, :] = zpad[:, :LANES]
        gcol[ti:tt, :] = zpad[:, :LANES]
    for h in range(GDN_HEADS):
        qh = y[:, h * GDN_DK:(h + 1) * GDN_DK]
        kh = y[:, nq + h * GDN_DK:nq + (h + 1) * GDN_DK]
        qs[0:ti, h * GDN_DK:(h + 1) * GDN_DK] = (qh * lax.rsqrt(jnp.sum(qh * qh, -1, keepdims=True) + NORM_EPS)
                                                 * (GDN_DK ** -0.5))
        ks[0:ti, h * GDN_DK:(h + 1) * GDN_DK] = kh * lax.rsqrt(jnp.sum(kh * kh, -1, keepdims=True) + NORM_EPS)
    vs[0:ti, :] = y[:, 2 * nq:]
    sm = sm_ref[...]
    bcol[0:ti, :] = jax.nn.sigmoid(sm)
    gcol[0:ti, :] = -jnp.exp(alr_ref[...]) * _softplus(sm + dtr_ref[...])

    ri = lax.broadcasted_iota(jnp.int32, (chunk, chunk), 0)
    ci = lax.broadcasted_iota(jnp.int32, (chunk, chunk), 1)
    causal = ri >= ci
    strict = ri > ci
    ltri = jnp.where(causal, 1.0, 0.0)
    utri = jnp.where(ri <= ci, 1.0, 0.0)
    col_live = lax.broadcasted_iota(jnp.int32, (8, chunk), 1) < ti
    n_levels = int(math.log2(chunk))

    for c in range(ncb):
        gcc[c * chunk:(c + 1) * chunk, :] = _mm(ltri, gcol[c * chunk:(c + 1) * chunk, :], NN, "f32")
        g_rows = -jnp.exp(alc_ref[...]) * _softplus(gt_ref[c] + dtc_ref[...])
        gcr[c] = _mm(jnp.where(col_live, g_rows, 0.0), utri, NN, "f32")

    cpi = GDN_CHUNKS_PER_STEP if ncb % GDN_CHUNKS_PER_STEP == 0 else 1

    def factor_step(c, carry):
        prob = range(cpi * GDN_HEADS)
        hd = [p % GDN_HEADS for p in prob]
        rows = [pl.ds(pl.multiple_of((c * cpi + p // GDN_HEADS) * chunk, chunk), chunk) for p in prob]
        hsl = [slice(h * GDN_DK, (h + 1) * GDN_DK) for h in hd]
        gcum_col = [gcc[rows[j * GDN_HEADS], :] for j in range(cpi)]
        gcum_row = [gcr[c * cpi + j] for j in range(cpi)]
        b_all = [bcol[rows[j * GDN_HEADS], :] for j in range(cpi)]
        kc = [ks[rows[p], hsl[p]] for p in prob]
        qc = [qs[rows[p], hsl[p]] for p in prob]
        beta = [b_all[p // GDN_HEADS][:, 8 + hd[p]:9 + hd[p]] for p in prob]
        gc = [gcum_col[p // GDN_HEADS][:, 12 + hd[p]:13 + hd[p]] for p in prob]
        eg = [jnp.exp(gc[p]) for p in prob]
        decay = [jnp.exp(jnp.where(causal, gc[p] - gcum_row[p // GDN_HEADS][4 + hd[p]:5 + hd[p], :], NEG_INF))
                 for p in prob]
        kb = [kc[p] * beta[p] for p in prob]
        gram = [_mm(jnp.concatenate([kb[p], qc[p]], axis=0), kc[p], NT, GDN_MM_GRAM) for p in prob]
        a = [-jnp.where(strict, gram[p][:chunk] * decay[p], 0.0) for p in prob]
        sol = [jnp.concatenate([vs[rows[p], hsl[p]] * beta[p], kb[p] * eg[p]], axis=1) for p in prob]
        for lvl in range(n_levels):
            mode = GDN_MM_SOLVE if lvl < GDN_SOLVE_FINE_LEVELS else "bf16"
            sol = [sol[p] + _mm(a[p], sol[p], NN, mode) for p in prob]
            if lvl + 1 < n_levels:
                a = [_mm(a[p], a[p], NN, "bf16") for p in prob]
        for p in prob:
            qks[hd[p], rows[p], :] = jnp.where(causal, gram[p][chunk:] * decay[p], 0.0)
            us[rows[p], hsl[p]] = sol[p][:, :GDN_DV]
            ws[rows[p], hsl[p]] = sol[p][:, GDN_DV:]
            qes[rows[p], hsl[p]] = qc[p] * eg[p]
            kds[rows[p], hsl[p]] = kc[p] * jnp.exp(gc[p][chunk - 1:chunk, :] - gc[p])
        return carry

    lax.fori_loop(0, ncb // cpi, factor_step, 0)

    def state_step(c, carry):
        r0 = pl.multiple_of(c * chunk, chunk)
        rows = pl.ds(r0, chunk)
        g_last_all = gcc[pl.ds(r0 + chunk - 1, 1), :]
        heads = range(GDN_HEADS)
        hsl = [slice(h * GDN_DK, (h + 1) * GDN_DK) for h in heads]
        s_prev = [s_scr[h][...] for h in heads]
        v_new = [us[rows, hsl[h]] - _mm(ws[rows, hsl[h]], s_prev[h], NN, GDN_MM_STATE) for h in heads]
        s_add = [_mm(kds[rows, hsl[h]], v_new[h], TN, GDN_MM_STATE) for h in heads]
        for h in heads:
            s_scr[h][...] = s_prev[h] * jnp.exp(g_last_all[:, 12 + h:13 + h]) + s_add[h]
        o = [_mm(qes[rows, hsl[h]], s_prev[h], NN, GDN_MM_STATE) + _mm(qks[h, rows, :], v_new[h], NN, GDN_MM_STATE)
             for h in heads]
        for h in heads:
            og = o[h] * lax.rsqrt(jnp.mean(o[h] * o[h], -1, keepdims=True) + NORM_EPS) * nw_ref[...]
            if ti < tt:
                o_ref[:, hsl[h]] = og[:ti] * _silu(z_ref[:, hsl[h]])
            else:
                o_ref[rows, hsl[h]] = og * _silu(z_ref[rows, hsl[h]])
        return carry

    lax.fori_loop(0, ncb, state_step, 0)

    @pl.when(i == n_i - 1)
    def _():
        for h in range(GDN_HEADS):
            sT_ref[0, h] = s_scr[h][...]


def _gdn(qkv, z, small, gates_t, conv_w, a_log, dt_bias, norm_w, conv0, s0, B, T):
    chunk = GDN_CHUNK
    ti = min(ROW_TILE, T)
    tt = max(ti, chunk)
    n_i = T // ti
    ncb = tt // chunk
    pad = (0, LANES - 16)
    alr = jnp.pad(jnp.concatenate([jnp.zeros((12,), F32), a_log]), pad).reshape(1, LANES)
    dtr = jnp.pad(jnp.concatenate([jnp.zeros((12,), F32), dt_bias]), pad).reshape(1, LANES)
    alc = jnp.concatenate([jnp.zeros((4,), F32), a_log]).reshape(8, 1)
    dtc = jnp.concatenate([jnp.zeros((4,), F32), dt_bias]).reshape(8, 1)
    nw = norm_w.reshape(1, GDN_DV)
    C = GDN_CONV_DIM
    W = GDN_WIDTH

    def const(shape):
        return pl.BlockSpec(shape, lambda b, i: (0,) * len(shape))

    o, s_fin, c_fin = pl.pallas_call(
        functools.partial(_gdn_body, ti=ti, tt=tt, chunk=chunk),
        grid=(B, n_i),
        in_specs=[
            pl.BlockSpec((ti, C), lambda b, i: (b * n_i + i, 0)),
            pl.BlockSpec((ti, W), lambda b, i: (b * n_i + i, 0)),
            pl.BlockSpec((ti, LANES), lambda b, i: (b * n_i + i, 0)),
            pl.BlockSpec((ncb, 8, chunk), lambda b, i: (b * n_i + i, 0, 0)),
            const((CONV_K, C)), const((1, LANES)), const((1, LANES)), const((8, 1)), const((8, 1)),
            const((1, GDN_DV)),
            pl.BlockSpec((1, CONV_K - 1, C), lambda b, i: (b, 0, 0)),
            pl.BlockSpec((1, GDN_HEADS, GDN_DK, GDN_DV), lambda b, i: (b, 0, 0, 0)),
        ],
        out_specs=[
            pl.BlockSpec((ti, W), lambda b, i: (b * n_i + i, 0)),
            pl.BlockSpec((1, GDN_HEADS, GDN_DK, GDN_DV), lambda b, i: (b, 0, 0, 0)),
            pl.BlockSpec((1, CONV_K - 1, C), lambda b, i: (b, 0, 0)),
        ],
        out_shape=[
            jax.ShapeDtypeStruct((B * T, W), F32),
            jax.ShapeDtypeStruct((B, GDN_HEADS, GDN_DK, GDN_DV), F32),
            jax.ShapeDtypeStruct((B, CONV_K - 1, C), F32),
        ],
        scratch_shapes=[
            pltpu.VMEM((ti + 8, C), F32),
            pltpu.VMEM((tt, W), F32), pltpu.VMEM((tt, W), F32), pltpu.VMEM((tt, W), F32),
            pltpu.VMEM((tt, LANES), F32), pltpu.VMEM((tt, LANES), F32),
            pltpu.VMEM((tt, LANES), F32), pltpu.VMEM((ncb, 8, chunk), F32),
            pltpu.VMEM((tt, W), F32), pltpu.VMEM((tt, W), F32),
            pltpu.VMEM((tt, W), F32), pltpu.VMEM((tt, W), F32),
            pltpu.VMEM((GDN_HEADS, tt, chunk), F32),
        ] + [pltpu.VMEM((GDN_DK, GDN_DV), F32)] * GDN_HEADS,
        compiler_params=_cparams(("parallel", "arbitrary")),
        name="gated_delta",
    )(qkv, z, small, gates_t, conv_w, alr, dtr, alc, dtc, nw, conv0, s0)
    return o, s_fin, c_fin


def _gates_transposed(small, B, T):
    chunk = GDN_CHUNK
    g = small[:, IDX_HEADS:IDX_HEADS + 2 * GDN_HEADS].reshape(B, T, 2 * GDN_HEADS)
    tpad = -(-T // chunk) * chunk
    g = jnp.pad(g, ((0, 0), (0, tpad - T), (0, 0)))
    return g.reshape(B * tpad // chunk, chunk, 2 * GDN_HEADS).transpose(0, 2, 1)


def _rope_tables(pos):
    d_rot = HEAD_DIM // 4
    half = d_rot // 2
    inv_freq = ROPE_THETA ** (-jnp.arange(half, dtype=F32) / half)
    ang = pos.astype(F32)[:, None] * inv_freq[None, :]
    cos = jnp.cos(ang)
    sin = jnp.sin(ang)
    n = pos.shape[0]
    ones = jnp.ones((n, HEAD_DIM - d_rot), F32)
    zeros = jnp.zeros((n, HEAD_DIM - d_rot), F32)
    zh = jnp.zeros((n, half), F32)
    c = jnp.concatenate([cos, cos, ones], axis=1)
    sa = jnp.concatenate([-sin, zh, zeros], axis=1)
    sb = jnp.concatenate([zh, sin, zeros], axis=1)
    rep = LANES // HEAD_DIM
    return jnp.tile(c, (1, rep)), jnp.tile(sa, (1, rep)), jnp.tile(sb, (1, rep))


def _split_w_in(w_in):
    pts = [0]
    for s in SPLITS:
        pts.append(pts[-1] + s)
    wq, wk, wv, wqi, wki, wwi, wqkv, wz, wb, wa = [w_in[:, pts[n]:pts[n + 1]] for n in range(len(SPLITS))]
    D = w_in.shape[0]
    wr = jnp.concatenate([wq, wqi, wk, wki, jnp.zeros((D, ROPE_W - 2 * ATT_WIDTH - LANES - IDX_DIM), F32)], axis=1)
    ws = jnp.concatenate([wwi, wb, wa, jnp.zeros((D, LANES - IDX_HEADS - 2 * GDN_HEADS), F32)], axis=1)
    return tuple(a.astype(BF16) for a in (wr, wv, ws, wqkv, wz))


def kernel(x_prompt, x_sample, cache_k, cache_v, cache_kidx, state_gdn, state_conv, page_table, ffn1_w_gate, ffn1_w_up, ffn1_w_down, ln1_g, ln1_b, w_in, conv_w, a_log, dt_bias, gdn_norm_w, w_o, ln2_g, ln2_b, ffn2_w_gate, ffn2_w_up, ffn2_w_down, ln3_g, ln3_b):
    B, S, D = x_prompt.shape
    Bs, T, _ = x_sample.shape
    n_pages = page_table.shape[1]
    page = cache_k.shape[2]
    past = n_pages * page
    kvw = N_KV_HEADS * HEAD_DIM
    l = 0
    xp = x_prompt.reshape(B * S, D)
    xs = x_sample.reshape(Bs * T, D)

    f1 = (ffn1_w_gate[l].astype(BF16), ffn1_w_up[l].astype(BF16), ffn1_w_down[l].astype(BF16), ln1_g, ln1_b)
    f2 = (ffn2_w_gate[l].astype(BF16), ffn2_w_up[l].astype(BF16), ffn2_w_down[l].astype(BF16), ln3_g, ln3_b)
    w_proj = _split_w_in(w_in[l])
    wo_att = w_o[l][:ATT_WIDTH].astype(BF16)
    wo_gdn = w_o[l][ATT_WIDTH:].astype(BF16)

    xp1 = _ffn_half_step(xp, *f1)
    tm_p = min(ROW_TILE, B * S)
    q, qi, kt, kit, vt, small, qkv, z, kt_bf, kit_bf, vt_bf = _mix_projection(
        xp1, w_proj, _rope_tables(jnp.arange(S)), S // tm_p, seq=S)
    att = _dsa_prompt(q, qi, small, kt_bf, kit_bf, vt_bf)
    gdn, s_p, c_p = _gdn(qkv, z, small, _gates_transposed(small, B, S), conv_w[l], a_log[l], dt_bias[l],
                         gdn_norm_w[l], jnp.zeros((B, CONV_K - 1, GDN_CONV_DIM), F32),
                         jnp.zeros((B, GDN_HEADS, GDN_DK, GDN_DV), F32), B, S)
    y_prompt = _ffn_half_step(xp1, *f2, mix=(att, gdn, wo_att, wo_gdn, ln2_g, ln2_b)).reshape(B, S, D)
    k_prompt = kt.reshape(1, B, N_KV_HEADS, HEAD_DIM, S).transpose(0, 1, 4, 2, 3)
    v_prompt = vt.reshape(1, B, N_KV_HEADS, HEAD_DIM, S).transpose(0, 1, 4, 2, 3)
    kidx_prompt = kit.reshape(1, B, IDX_DIM, S).transpose(0, 1, 3, 2)

    xs1 = _ffn_half_step(xs, *f1)
    pos_s = jnp.tile(past + jnp.arange(T), Bs)
    tm_s = min(ROW_TILE, Bs * T)
    q, qi, k, ki, v, small, qkv, z = _mix_projection(xs1, w_proj, _rope_tables(pos_s), Bs * T // tm_s)
    group = ATT_HEADS // N_KV_HEADS
    qh = q.reshape(Bs, T, N_KV_HEADS, group, HEAD_DIM).transpose(0, 2, 3, 1, 4)
    zq = jnp.zeros_like(qh[:, 0])
    q_ht = jnp.concatenate([jnp.concatenate([qh[:, 0], zq], -1), jnp.concatenate([zq, qh[:, 1]], -1)], axis=1)
    q_ht = q_ht.reshape(Bs, ATT_HEADS * T, LANES)
    qi_ht = qi.reshape(Bs, T, IDX_HEADS, IDX_DIM).transpose(0, 2, 1, 3).reshape(Bs, IDX_HEADS * T, IDX_DIM)

    def new_tile(a):
        a = a.reshape(Bs, T, a.shape[-1]).transpose(0, 2, 1)
        return jnp.pad(a, ((0, 0), (0, 0), (0, LANES - T)))

    cki_t = cache_kidx[l].transpose(0, 2, 1)
    ck_t = cache_k[l].transpose(0, 2, 3, 1).reshape(-1, kvw, page)
    cv_t = cache_v[l].transpose(0, 2, 3, 1).reshape(-1, kvw, page)
    o_ht = _dsa_sample(page_table, qi_ht, small.reshape(Bs, T, LANES), q_ht, new_tile(ki), new_tile(k), new_tile(v),
                       cki_t, ck_t, cv_t)
    o_ht = o_ht.reshape(Bs, N_KV_HEADS, group, T, N_KV_HEADS, HEAD_DIM)
    att = jnp.stack([o_ht[:, 0, :, :, 0], o_ht[:, 1, :, :, 1]], axis=1)
    att = att.transpose(0, 3, 1, 2, 4).reshape(Bs * T, ATT_WIDTH)
    gdn, s_s, c_s = _gdn(qkv, z, small, _gates_transposed(small, Bs, T), conv_w[l], a_log[l], dt_bias[l],
                         gdn_norm_w[l], state_conv[l], state_gdn[l], Bs, T)
    y_sample = _ffn_half_step(xs1, *f2, mix=(att, gdn, wo_att, wo_gdn, ln2_g, ln2_b)).reshape(Bs, T, D)

    return (y_prompt, y_sample,
            k_prompt, v_prompt, kidx_prompt, s_p[None], c_p[None],
            k.reshape(1, Bs, T, N_KV_HEADS, HEAD_DIM), v.reshape(1, Bs, T, N_KV_HEADS, HEAD_DIM),
            ki.reshape(1, Bs, T, IDX_DIM), s_s[None], c_s[None])
```

```python
import functools
import math

import jax
import jax.numpy as jnp
from jax import lax
from jax.experimental import pallas as pl
from jax.experimental.pallas import tpu as pltpu

F32 = jnp.float32
BF16 = jnp.bfloat16
HI = lax.Precision.HIGHEST

LANES = 128
VMEM_LIMIT = 56 * 1024 * 1024

D_MODEL = 1024
DEPTH = 1
ATT_HEADS = 8
N_KV_HEADS = 2
HEAD_DIM = 64
ATT_WIDTH = ATT_HEADS * HEAD_DIM
ROPE_THETA = 500000.0
IDX_HEADS = 8
IDX_DIM = 64
TOPK_MAX = 256
GDN_HEADS = 4
GDN_DK = 128
GDN_DV = 128
GDN_WIDTH = GDN_HEADS * GDN_DV
GDN_CONV_DIM = 2 * GDN_HEADS * GDN_DK + GDN_WIDTH
CONV_K = 4
GDN_CHUNK = 64
LN_EPS = 1e-5
NORM_EPS = 1e-6
DEEP_ALPHA = (2 * DEPTH) ** 0.25
SPLITS = (ATT_WIDTH, N_KV_HEADS * HEAD_DIM, N_KV_HEADS * HEAD_DIM, IDX_HEADS * IDX_DIM, IDX_DIM, IDX_HEADS,
          GDN_CONV_DIM, GDN_WIDTH, GDN_HEADS, GDN_HEADS)
ROPE_W = 1280
TOPK_SEARCH_BINARY = (20, 2)
TOPK_SEARCH_WIDE = (10, 4)
TOPK_WIDE_MAX_ELEMS = 128 * 1024

NEG_INF = float("-inf")
POS_INF = float("inf")
NN = (((1,), (0,)), ((), ()))
NT = (((1,), (1,)), ((), ()))
TN = (((0,), (0,)), ((), ()))
ROW_TILE = 512
QUERY_TILE = 128
IDX_KEY_CHUNK = 256
CAUSAL_CLASSES = 4
GDN_MM_GRAM = "bf16"
GDN_MM_SOLVE = "x3"
GDN_SOLVE_FINE_LEVELS = 2
GDN_CHUNKS_PER_STEP = 4
GDN_MM_STATE = "bf16"


def _cparams(sem):
    return pltpu.CompilerParams(dimension_semantics=sem, vmem_limit_bytes=VMEM_LIMIT)


def _layer_norm_rows(y, g, b):
    mu = jnp.mean(y, axis=-1, keepdims=True)
    d = y - mu
    var = jnp.mean(d * d, axis=-1, keepdims=True)
    return d * lax.rsqrt(var + LN_EPS) * g + b


def _silu(x):
    return x * jax.nn.sigmoid(x)


def _softplus(x):
    return jnp.maximum(x, 0.0) + jnp.log(1.0 + jnp.exp(-jnp.abs(x)))


def _split_bf16(x):
    hi = x.astype(BF16)
    return hi, (x - hi.astype(F32)).astype(BF16)


def _mm(a, b, dims, mode):
    if mode == "f32":
        return lax.dot_general(a, b, dims, precision=HI, preferred_element_type=F32)
    dot = functools.partial(lax.dot_general, dimension_numbers=dims, preferred_element_type=F32)
    if mode == "bf16":
        return dot(a.astype(BF16), b.astype(BF16))
    ah, al = _split_bf16(a)
    bh, bl = _split_bf16(b)
    return dot(ah, bh) + (dot(ah, bl) + dot(al, bh))


def _ffn_body(*refs, mixed):
    j = pl.program_id(1)
    if mixed:
        (x_in_ref, att_ref, gdn_ref, woa_ref, wog_ref, g2_ref, b2_ref,
         wg_ref, wu_ref, wd_ref, g_ref, b_ref, o_ref, acc_ref, x_ref) = refs

        @pl.when(j == 0)
        def _():
            mix = (jnp.dot(att_ref[...].astype(BF16), woa_ref[...], preferred_element_type=F32)
                   + jnp.dot(gdn_ref[...].astype(BF16), wog_ref[...], preferred_element_type=F32))
            x_ref[...] = _layer_norm_rows(DEEP_ALPHA * x_in_ref[...] + mix, g2_ref[...], b2_ref[...])
    else:
        x_ref, wg_ref, wu_ref, wd_ref, g_ref, b_ref, o_ref, acc_ref = refs
    xb = x_ref[...].astype(BF16)
    hg = jnp.dot(xb, wg_ref[...], preferred_element_type=F32)
    hu = jnp.dot(xb, wu_ref[...], preferred_element_type=F32)
    h = _silu(hg) * hu
    part = jnp.dot(h.astype(BF16), wd_ref[...], preferred_element_type=F32)

    @pl.when(j == 0)
    def _():
        acc_ref[...] = part

    @pl.when(j > 0)
    def _():
        acc_ref[...] += part

    @pl.when(j == pl.num_programs(1) - 1)
    def _():
        y = DEEP_ALPHA * x_ref[...] + 0.5 * acc_ref[...]
        o_ref[...] = _layer_norm_rows(y, g_ref[...], b_ref[...])


def _ffn_half_step(x, wg, wu, wd, g, b, mix=None):
    M, D = x.shape
    FF = wg.shape[1]
    tm = min(ROW_TILE, M)
    tf = FF // 2 if (FF // 2) % LANES == 0 else FF

    def rows(width):
        return pl.BlockSpec((tm, width), lambda i, j: (i, 0))

    def full(a):
        return pl.BlockSpec(a.shape, lambda i, j: (0, 0))

    ffn_specs = [
        pl.BlockSpec((D, tf), lambda i, j: (0, j)),
        pl.BlockSpec((D, tf), lambda i, j: (0, j)),
        pl.BlockSpec((tf, D), lambda i, j: (j, 0)),
        full(g), full(b),
    ]
    scratch = [pltpu.VMEM((tm, D), F32)]
    if mix is None:
        operands = (x, wg, wu, wd, g, b)
        in_specs = [rows(D)] + ffn_specs
    else:
        att, gdn, wo_att, wo_gdn, g2, b2 = mix
        operands = (x, att, gdn, wo_att, wo_gdn, g2, b2, wg, wu, wd, g, b)
        in_specs = [rows(D), rows(att.shape[1]), rows(gdn.shape[1]), full(wo_att), full(wo_gdn), full(g2),
                    full(b2)] + ffn_specs
        scratch.append(pltpu.VMEM((tm, D), F32))
    return pl.pallas_call(
        functools.partial(_ffn_body, mixed=mix is not None),
        grid=(M // tm, FF // tf),
        in_specs=in_specs,
        out_specs=rows(D),
        out_shape=jax.ShapeDtypeStruct((M, D), F32),
        scratch_shapes=scratch,
        compiler_params=_cparams(("parallel", "arbitrary")),
        name="ffn_half_step",
    )(*operands)


def _proj_body(x_ref, wr_ref, wv_ref, ws_ref, wqkv_ref, wz_ref, cos_ref, sa_ref, sb_ref,
               q_ref, qi_ref, k_ref, ki_ref, v_ref, small_ref, qkv_ref, z_ref, *bf16_refs, transposed):
    xb = x_ref[...].astype(BF16)
    r = jnp.dot(xb, wr_ref[...], preferred_element_type=F32)
    c = cos_ref[...]
    sa = sa_ref[...]
    sb = sb_ref[...]

    def rope(slab):
        return slab * c + pltpu.roll(slab, LANES - 8, 1) * sa + pltpu.roll(slab, 8, 1) * sb

    for s in range(ATT_WIDTH // LANES):
        q_ref[:, s * LANES:(s + 1) * LANES] = rope(r[:, s * LANES:(s + 1) * LANES])
    off = ATT_WIDTH
    for s in range(IDX_HEADS * IDX_DIM // LANES):
        qi_ref[:, s * LANES:(s + 1) * LANES] = rope(r[:, off + s * LANES:off + (s + 1) * LANES])
    off += IDX_HEADS * IDX_DIM
    kr = rope(r[:, off:off + LANES])
    kir = rope(r[:, off + LANES:off + 2 * LANES])
    v = jnp.dot(xb, wv_ref[...], preferred_element_type=F32)
    if transposed:
        kb_ref, kib_ref, vb_ref = bf16_refs
        kt = kr.T
        kit = kir.T[:IDX_DIM, :]
        vt = v.T
        k_ref[0] = kt
        ki_ref[0] = kit
        v_ref[0] = vt
        kb_ref[0] = kt.astype(BF16)
        kib_ref[0] = kit.astype(BF16)
        vb_ref[0] = vt.astype(BF16)
    else:
        k_ref[...] = kr
        ki_ref[...] = kir[:, :IDX_DIM]
        v_ref[...] = v
    sm = jnp.dot(xb, ws_ref[...], preferred_element_type=F32)
    lane = lax.broadcasted_iota(jnp.int32, sm.shape, 1)
    small_ref[...] = jnp.where(lane < IDX_HEADS, sm * (IDX_HEADS ** -0.5), sm)
    qkv_ref[...] = jnp.dot(xb, wqkv_ref[...], preferred_element_type=F32)
    z_ref[...] = jnp.dot(xb, wz_ref[...], preferred_element_type=F32)


def _mix_projection(x, w, tabs, n_tab_blocks, seq=None):
    M, D = x.shape
    tm = min(ROW_TILE, M)
    wr, wv, ws, wqkv, wz = w
    cos_t, sa_t, sb_t = tabs
    kvw = N_KV_HEADS * HEAD_DIM

    def full(a):
        return pl.BlockSpec(a.shape, lambda i: (0, 0))

    def rows(width):
        return pl.BlockSpec((tm, width), lambda i: (i, 0))

    tab_spec = pl.BlockSpec((tm, LANES), lambda i: (i % n_tab_blocks, 0))
    if seq is None:
        kv_shapes = [jax.ShapeDtypeStruct((M, kvw), F32), jax.ShapeDtypeStruct((M, IDX_DIM), F32),
                     jax.ShapeDtypeStruct((M, kvw), F32)]
        kv_specs = [rows(kvw), rows(IDX_DIM), rows(kvw)]
    else:
        nt = seq // tm
        nb = M // seq

        def tr(width):
            return pl.BlockSpec((1, width, tm), lambda i: (i // nt, 0, i % nt))

        kv_shapes = [jax.ShapeDtypeStruct((nb, kvw, seq), F32), jax.ShapeDtypeStruct((nb, IDX_DIM, seq), F32),
                     jax.ShapeDtypeStruct((nb, kvw, seq), F32)]
        kv_specs = [tr(kvw), tr(IDX_DIM), tr(kvw)]
    out_shapes = [
        jax.ShapeDtypeStruct((M, ATT_WIDTH), F32),
        jax.ShapeDtypeStruct((M, IDX_HEADS * IDX_DIM), F32),
    ] + kv_shapes + [
        jax.ShapeDtypeStruct((M, LANES), F32),
        jax.ShapeDtypeStruct((M, GDN_CONV_DIM), F32),
        jax.ShapeDtypeStruct((M, GDN_WIDTH), F32),
    ]
    out_specs = [rows(ATT_WIDTH), rows(IDX_HEADS * IDX_DIM)] + kv_specs + [
        rows(LANES), rows(GDN_CONV_DIM), rows(GDN_WIDTH)]
    if seq is not None:
        out_shapes += [jax.ShapeDtypeStruct(s.shape, BF16) for s in kv_shapes]
        out_specs += kv_specs
    return pl.pallas_call(
        functools.partial(_proj_body, transposed=seq is not None),
        grid=(M // tm,),
        in_specs=[rows(D), full(wr), full(wv), full(ws), full(wqkv), full(wz), tab_spec, tab_spec, tab_spec],
        out_specs=out_specs,
        out_shape=out_shapes,
        compiler_params=_cparams(("parallel",)),
        name="mix_projection",
    )(x, wr, wv, ws, wqkv, wz, cos_t, sa_t, sb_t)


def _count_ge(xm, thr):
    return jnp.sum(jnp.where(xm >= thr, 1.0, 0.0), axis=-1, keepdims=True)


def _prefix_count(eq_f32):
    R, L = eq_f32.shape
    ri = lax.broadcasted_iota(jnp.int32, (LANES, LANES), 0)
    ci = lax.broadcasted_iota(jnp.int32, (LANES, LANES), 1)
    tri = jnp.where(ri <= ci, 1.0, 0.0).astype(BF16)
    offset = jnp.zeros((R, 1), F32)
    pieces = []
    for blk in range(L // LANES):
        e = eq_f32[:, blk * LANES:(blk + 1) * LANES].astype(BF16)
        loc = jnp.dot(e, tri, preferred_element_type=F32)
        pieces.append(loc + offset)
        offset = offset + loc[:, LANES - 1:LANES]
    return jnp.concatenate(pieces, axis=1)


def _topk_bias(xms, k, n_steps, arity, bounds=None):
    kf = float(k)
    n = len(xms)
    grp = range(n)
    if bounds is None:
        rowmax = [jnp.max(xm, axis=-1, keepdims=True) for xm in xms]
        lo0 = [jnp.min(jnp.where(xm == NEG_INF, POS_INF, xm), axis=-1, keepdims=True) for xm in xms]
    else:
        lo0 = [b[0] for b in bounds]
        rowmax = [b[1] for b in bounds]
    hi0 = [m + jnp.abs(m) * (2.0 ** -10) + 1.0 for m in rowmax]

    def search(_, c):
        lo, hi = c[:n], c[n:]
        mids = [[lo[g] + (hi[g] - lo[g]) * (j / arity) for j in range(1, arity)] for g in grp]
        ge = [[_count_ge(xms[g], m) >= kf for m in mids[g]] for g in grp]
        new_lo, new_hi = [], []
        for g in grp:
            nlo, nhi = lo[g], hi[g]
            for j in range(arity - 1):
                nlo = jnp.where(ge[g][j], mids[g][j], nlo)
            for j in reversed(range(arity - 1)):
                nhi = jnp.where(ge[g][j], nhi, mids[g][j])
            new_lo.append(nlo)
            new_hi.append(nhi)
        return tuple(new_lo) + tuple(new_hi)

    c = lax.fori_loop(0, n_steps, search, tuple(lo0) + tuple(hi0))
    lo, hi = list(c[:n]), list(c[n:])
    c_lo = [_count_ge(xms[g], lo[g]) for g in grp]
    pending = [jnp.where(c_lo[g] > kf, 1.0, 0.0) for g in grp]

    def cond(c):
        worst = jnp.max(c[3 * n])
        for g in range(1, n):
            worst = jnp.maximum(worst, jnp.max(c[3 * n + g]))
        return worst > 0.0

    def body(c):
        lo, hi, c_lo, pending = c[:n], c[n:2 * n], c[2 * n:3 * n], c[3 * n:]
        t = [jnp.max(jnp.where(xms[g] < hi[g], xms[g], NEG_INF), axis=-1, keepdims=True) for g in grp]
        ct = [_count_ge(xms[g], t[g]) for g in grp]
        out = [[], [], [], []]
        for g in grp:
            hit = ct[g] >= kf
            live = pending[g] > 0.0
            upd = jnp.logical_and(live, hit)
            out[0].append(jnp.where(upd, t[g], lo[g]))
            out[1].append(jnp.where(jnp.logical_and(live, jnp.logical_not(hit)), t[g], hi[g]))
            out[2].append(jnp.where(upd, ct[g], c_lo[g]))
            out[3].append(jnp.where(hit, 0.0, pending[g]))
        return tuple(out[0]) + tuple(out[1]) + tuple(out[2]) + tuple(out[3])

    c = lax.while_loop(cond, body, tuple(lo) + tuple(hi) + tuple(c_lo) + tuple(pending))
    lo, c_lo = c[:n], c[2 * n:3 * n]

    def select(xm, lo, c_lo):
        def plain():
            return jnp.where(xm >= lo, 0.0, NEG_INF)

        def with_ties():
            gt = xm > lo
            eq = xm == lo
            neg_zero = jnp.logical_and(xm == 0.0, 1.0 / xm < 0.0)
            eq_hi = jnp.logical_and(eq, jnp.logical_not(neg_zero))
            eq_lo = jnp.logical_and(eq, neg_zero)
            room = kf - jnp.sum(jnp.where(gt, 1.0, 0.0), axis=-1, keepdims=True)
            rank_hi = _prefix_count(jnp.where(eq_hi, 1.0, 0.0))
            rank_lo = _prefix_count(jnp.where(eq_lo, 1.0, 0.0)) + rank_hi[:, -1:]
            keep = jnp.logical_or(gt, jnp.logical_or(jnp.logical_and(eq_hi, rank_hi <= room),
                                                     jnp.logical_and(eq_lo, rank_lo <= room)))
            return jnp.where(keep, 0.0, NEG_INF)

        return lax.cond(jnp.max(c_lo) > kf, with_ties, plain)

    return [select(xms[g], lo[g], c_lo[g]) for g in grp]


def _indexer_scores(qi_bf, kit_ref, wi, sk):
    ws = wi * (IDX_DIM ** -0.5)
    pieces = []
    lo_run = hi_run = None
    for kc in range(sk // IDX_KEY_CHUNK):
        kit = kit_ref[0, :, kc * IDX_KEY_CHUNK:(kc + 1) * IDX_KEY_CHUNK]
        acc = None
        for h in range(IDX_HEADS):
            s = jnp.dot(qi_bf[:, h * IDX_DIM:(h + 1) * IDX_DIM], kit, preferred_element_type=F32)
            term = jnp.maximum(s, 0.0) * ws[:, h:h + 1]
            acc = term if acc is None else acc + term
        pieces.append(acc)
        for j in range(IDX_KEY_CHUNK // LANES):
            part = acc[:, j * LANES:(j + 1) * LANES]
            lo_run = part if lo_run is None else jnp.minimum(lo_run, part)
            hi_run = part if hi_run is None else jnp.maximum(hi_run, part)
    return (jnp.concatenate(pieces, axis=1), jnp.min(lo_run, axis=-1, keepdims=True),
            jnp.max(hi_run, axis=-1, keepdims=True))


def _dsa_prompt_block(q_ref, qi_ref, sm_ref, kt_ref, kit_ref, vt_ref, o_ref, i, *, tq, ksel, sk, dense):
    q_pos = i * tq + lax.broadcasted_iota(jnp.int32, (tq, sk), 0)
    k_pos = lax.broadcasted_iota(jnp.int32, (tq, sk), 1)
    if dense:
        bias = jnp.where(k_pos <= q_pos, 0.0, NEG_INF)
    else:
        score, smin, smax = _indexer_scores(qi_ref[...].astype(BF16), kit_ref, sm_ref[...], sk)
        xm = jnp.where(k_pos <= q_pos, score, NEG_INF)
        search = TOPK_SEARCH_WIDE if tq * sk <= TOPK_WIDE_MAX_ELEMS else TOPK_SEARCH_BINARY
        bias = _topk_bias([xm], ksel, *search, bounds=[(smin, smax)])[0]

    kt = kt_ref[0, :, :sk]
    vt = vt_ref[0, :, :sk]
    qb = (q_ref[...] * (HEAD_DIM ** -0.5)).astype(BF16)
    group = ATT_HEADS // N_KV_HEADS
    lane = lax.broadcasted_iota(jnp.int32, (tq, LANES), 1)
    outs = []
    for h0 in range(0, ATT_HEADS, group):
        hs = range(h0, h0 + group)
        kt_g = kt[(h0 // group) * HEAD_DIM:(h0 // group + 1) * HEAD_DIM, :]
        s = [jnp.dot(qb[:, h * HEAD_DIM:(h + 1) * HEAD_DIM], kt_g, preferred_element_type=F32) + bias for h in hs]
        m = [jnp.max(x, axis=-1, keepdims=True) for x in s]
        p = [jnp.exp(x - mx) for x, mx in zip(s, m)]
        l = [jnp.sum(x, axis=-1, keepdims=True) for x in p]
        outs += [lax.dot_general(x.astype(BF16), vt, NT, preferred_element_type=F32) / lx
                 for x, lx in zip(p, l)]
    for pair in range(ATT_HEADS // 2):
        h0, h1 = 2 * pair, 2 * pair + 1
        kv = h0 // group
        a, b = outs[h0], outs[h1]
        if kv == 0:
            b = pltpu.roll(b, HEAD_DIM, 1)
        else:
            a = pltpu.roll(a, HEAD_DIM, 1)
        o_ref[:, pair * LANES:(pair + 1) * LANES] = jnp.where(lane < HEAD_DIM, a, b)


def _dsa_prompt_body(q_ref, qi_ref, sm_ref, kt_ref, kit_ref, vt_ref, o_ref, *, tq, ksel, n_classes):
    i = pl.program_id(1)
    per = kt_ref.shape[2] // tq // n_classes
    n_dense = min(ksel // tq, per)
    if n_dense:
        @pl.when(i < n_dense)
        def _():
            _dsa_prompt_block(q_ref, qi_ref, sm_ref, kt_ref, kit_ref, vt_ref, o_ref, i,
                              tq=tq, ksel=ksel, sk=n_dense * tq, dense=True)
    for c in range(n_classes):
        @pl.when(jnp.logical_and(i >= max(c * per, n_dense), i < (c + 1) * per))
        def _(c=c):
            _dsa_prompt_block(q_ref, qi_ref, sm_ref, kt_ref, kit_ref, vt_ref, o_ref, i,
                              tq=tq, ksel=ksel, sk=(c + 1) * per * tq, dense=False)


def _dsa_prompt(q, qi, small, kt, kit, vt):
    B, kvw, S = kt.shape
    tq = QUERY_TILE
    nq = S // tq
    ksel = min(TOPK_MAX, S // 4)
    n_classes = CAUSAL_CLASSES if nq % CAUSAL_CLASSES == 0 and (nq // CAUSAL_CLASSES * tq) % IDX_KEY_CHUNK == 0 else 1
    return pl.pallas_call(
        functools.partial(_dsa_prompt_body, tq=tq, ksel=ksel, n_classes=n_classes),
        grid=(B, nq),
        in_specs=[
            pl.BlockSpec((tq, ATT_WIDTH), lambda b, i: (b * nq + i, 0)),
            pl.BlockSpec((tq, IDX_HEADS * IDX_DIM), lambda b, i: (b * nq + i, 0)),
            pl.BlockSpec((tq, LANES), lambda b, i: (b * nq + i, 0)),
            pl.BlockSpec((1, kvw, S), lambda b, i: (b, 0, 0)),
            pl.BlockSpec((1, IDX_DIM, S), lambda b, i: (b, 0, 0)),
            pl.BlockSpec((1, kvw, S), lambda b, i: (b, 0, 0)),
        ],
        out_specs=pl.BlockSpec((tq, ATT_WIDTH), lambda b, i: (b * nq + i, 0)),
        out_shape=jax.ShapeDtypeStruct((B * S, ATT_WIDTH), F32),
        compiler_params=_cparams(("parallel", "arbitrary")),
        name="dsa_prompt",
    )(q, qi, small, kt, kit, vt)


def _dsa_sample_body(pt_ref, qi_ref, wi_ref, q_ref, kin_ref, kn_ref, vn_ref, cki_hbm, ck_hbm, cv_hbm, o_ref,
                     kibuf, kbuf, vbuf, sem, *, n_pages, page, t_new, ksel):
    b = pl.program_id(0)
    nb = pl.num_programs(0)
    slot = lax.rem(b, 2)
    past = n_pages * page
    lp = kibuf.shape[2]

    def page_copies(bb, sl, p):
        pg = pt_ref[bb, p]
        cols = pl.ds(p * page, page)
        return (pltpu.make_async_copy(cki_hbm.at[pg], kibuf.at[sl, :, cols], sem.at[sl, 0]),
                pltpu.make_async_copy(ck_hbm.at[pg], kbuf.at[sl, :, cols], sem.at[sl, 1]),
                pltpu.make_async_copy(cv_hbm.at[pg], vbuf.at[sl, :, cols], sem.at[sl, 2]))

    def start_all(bb, sl):
        for p in range(n_pages):
            for cp in page_copies(bb, sl, p):
                cp.start()

    def wait_all(bb, sl):
        for p in range(n_pages):
            for cp in page_copies(bb, sl, p):
                cp.wait()

    @pl.when(b == 0)
    def _():
        start_all(0, 0)

    @pl.when(b + 1 < nb)
    def _():
        start_all(b + 1, 1 - slot)

    kibuf[slot, :, past:lp] = kin_ref[0]
    kbuf[slot, :, past:lp] = kn_ref[0]
    vbuf[slot, :, past:lp] = vn_ref[0]

    wait_all(b, slot)

    s_idx = jnp.dot(qi_ref[0].astype(BF16), kibuf[slot].astype(BF16),
                    preferred_element_type=F32)
    wi = wi_ref[0] * (IDX_DIM ** -0.5)
    score = None
    for h in range(IDX_HEADS):
        term = jnp.maximum(s_idx[h * t_new:(h + 1) * t_new, :], 0.0) * wi[:, h:h + 1]
        score = term if score is None else score + term
    q_pos = past + lax.broadcasted_iota(jnp.int32, (t_new, lp), 0)
    k_pos = lax.broadcasted_iota(jnp.int32, (t_new, lp), 1)
    bias = _topk_bias([jnp.where(k_pos <= q_pos, score, NEG_INF)], ksel, *TOPK_SEARCH_WIDE)[0]

    qb = (q_ref[0] * (HEAD_DIM ** -0.5)).astype(BF16)
    s = jnp.dot(qb, kbuf[slot].astype(BF16), preferred_element_type=F32)
    s = s + jnp.concatenate([bias] * ATT_HEADS, axis=0)
    m = jnp.max(s, axis=-1, keepdims=True)
    p = jnp.exp(s - m)
    l = jnp.sum(p, axis=-1, keepdims=True)
    o_ref[0] = lax.dot_general(p.astype(BF16), vbuf[slot].astype(BF16), NT, preferred_element_type=F32) / l


def _dsa_sample(page_table, qi_ht, wi, q_ht, kit_new, kt_new, vt_new, cki_t, ck_t, cv_t):
    B, n_pages = page_table.shape
    page = cki_t.shape[2]
    t_new = wi.shape[1]
    past = n_pages * page
    lp = past + LANES
    ksel = min(TOPK_MAX, (past + t_new) // 4)
    rows = ATT_HEADS * t_new
    kvw = N_KV_HEADS * HEAD_DIM
    grid_spec = pltpu.PrefetchScalarGridSpec(
        num_scalar_prefetch=1,
        grid=(B,),
        in_specs=[
            pl.BlockSpec((1, rows, IDX_DIM), lambda b, pt: (b, 0, 0)),
            pl.BlockSpec((1, t_new, LANES), lambda b, pt: (b, 0, 0)),
            pl.BlockSpec((1, rows, LANES), lambda b, pt: (b, 0, 0)),
            pl.BlockSpec((1, IDX_DIM, LANES), lambda b, pt: (b, 0, 0)),
            pl.BlockSpec((1, kvw, LANES), lambda b, pt: (b, 0, 0)),
            pl.BlockSpec((1, kvw, LANES), lambda b, pt: (b, 0, 0)),
            pl.BlockSpec(memory_space=pl.ANY),
            pl.BlockSpec(memory_space=pl.ANY),
            pl.BlockSpec(memory_space=pl.ANY),
        ],
        out_specs=pl.BlockSpec((1, rows, LANES), lambda b, pt: (b, 0, 0)),
        scratch_shapes=[
            pltpu.VMEM((2, IDX_DIM, lp), F32),
            pltpu.VMEM((2, kvw, lp), F32),
            pltpu.VMEM((2, kvw, lp), F32),
            pltpu.SemaphoreType.DMA((2, 3)),
        ],
    )
    return pl.pallas_call(
        functools.partial(_dsa_sample_body, n_pages=n_pages, page=page, t_new=t_new, ksel=ksel),
        grid_spec=grid_spec,
        out_shape=jax.ShapeDtypeStruct((B, rows, LANES), F32),
        compiler_params=_cparams(("arbitrary",)),
        name="dsa_sample",
    )(page_table, qi_ht, wi, q_ht, kit_new, kt_new, vt_new, cki_t, ck_t, cv_t)


def _gdn_body(qkv_ref, z_ref, sm_ref, gt_ref, cw_ref, alr_ref, dtr_ref, alc_ref, dtc_ref, nw_ref, c0_ref, s0_ref,
              o_ref, sT_ref, cT_ref, xbuf, qs, ks, vs, bcol, gcol, gcc, gcr, us, ws, qes, kds, qks,
              s0_scr, s1_scr, s2_scr, s3_scr, *, ti, tt, chunk):
    i = pl.program_id(1)
    n_i = pl.num_programs(1)
    nq = GDN_HEADS * GDN_DK
    ncb = tt // chunk
    halo = 8
    s_scr = (s0_scr, s1_scr, s2_scr, s3_scr)

    @pl.when(i == 0)
    def _():
        for h in range(GDN_HEADS):
            s_scr[h][...] = s0_ref[0, h]
        xbuf[halo - (CONV_K - 1):halo, :] = c0_ref[0]

    xbuf[halo:halo + ti, :] = qkv_ref[...]
    new_tail = xbuf[halo + ti - (CONV_K - 1):halo + ti, :]
    cw = cw_ref[...]
    y = None
    for j in range(CONV_K):
        term = xbuf[halo - (CONV_K - 1) + j:halo - (CONV_K - 1) + j + ti, :] * cw[j:j + 1, :]
        y = term if y is None else y + term
    y = _silu(y)
    xbuf[halo - (CONV_K - 1):halo, :] = new_tail

    @pl.when(i == n_i - 1)
    def _():
        cT_ref[0] = new_tail

    if ti < tt:
        zpad = jnp.zeros((tt - ti, nq), F32)
        qs[ti:tt, :] = zpad
        ks[ti:tt, :] = zpad
        vs[ti:tt, :] = zpad
        bcol[ti:tt, :] = zpad[:, :LANES]
        gcol[ti:tt, :] = zpad[:, :LANES]
    for h in range(GDN_HEADS):
        qh = y[:, h * GDN_DK:(h + 1) * GDN_DK]
        kh = y[:, nq + h * GDN_DK:nq + (h + 1) * GDN_DK]
        qs[0:ti, h * GDN_DK:(h + 1) * GDN_DK] = (qh * lax.rsqrt(jnp.sum(qh * qh, -1, keepdims=True) + NORM_EPS)
                                                 * (GDN_DK ** -0.5))
        ks[0:ti, h * GDN_DK:(h + 1) * GDN_DK] = kh * lax.rsqrt(jnp.sum(kh * kh, -1, keepdims=True) + NORM_EPS)
    vs[0:ti, :] = y[:, 2 * nq:]
    sm = sm_ref[...]
    bcol[0:ti, :] = jax.nn.sigmoid(sm)
    gcol[0:ti, :] = -jnp.exp(alr_ref[...]) * _softplus(sm + dtr_ref[...])

    ri = lax.broadcasted_iota(jnp.int32, (chunk, chunk), 0)
    ci = lax.broadcasted_iota(jnp.int32, (chunk, chunk), 1)
    causal = ri >= ci
    strict = ri > ci
    ltri = jnp.where(causal, 1.0, 0.0)
    utri = jnp.where(ri <= ci, 1.0, 0.0)
    col_live = lax.broadcasted_iota(jnp.int32, (8, chunk), 1) < ti
    n_levels = int(math.log2(chunk))

    for c in range(ncb):
        gcc[c * chunk:(c + 1) * chunk, :] = _mm(ltri, gcol[c * chunk:(c + 1) * chunk, :], NN, "f32")
        g_rows = -jnp.exp(alc_ref[...]) * _softplus(gt_ref[c] + dtc_ref[...])
        gcr[c] = _mm(jnp.where(col_live, g_rows, 0.0), utri, NN, "f32")

    cpi = GDN_CHUNKS_PER_STEP if ncb % GDN_CHUNKS_PER_STEP == 0 else 1

    def factor_step(c, carry):
        prob = range(cpi * GDN_HEADS)
        hd = [p % GDN_HEADS for p in prob]
        rows = [pl.ds(pl.multiple_of((c * cpi + p // GDN_HEADS) * chunk, chunk), chunk) for p in prob]
        hsl = [slice(h * GDN_DK, (h + 1) * GDN_DK) for h in hd]
        gcum_col = [gcc[rows[j * GDN_HEADS], :] for j in range(cpi)]
        gcum_row = [gcr[c * cpi + j] for j in range(cpi)]
        b_all = [bcol[rows[j * GDN_HEADS], :] for j in range(cpi)]
        kc = [ks[rows[p], hsl[p]] for p in prob]
        qc = [qs[rows[p], hsl[p]] for p in prob]
        beta = [b_all[p // GDN_HEADS][:, 8 + hd[p]:9 + hd[p]] for p in prob]
        gc = [gcum_col[p // GDN_HEADS][:, 12 + hd[p]:13 + hd[p]] for p in prob]
        eg = [jnp.exp(gc[p]) for p in prob]
        decay = [jnp.exp(jnp.where(causal, gc[p] - gcum_row[p // GDN_HEADS][4 + hd[p]:5 + hd[p], :], NEG_INF))
                 for p in prob]
        kb = [kc[p] * beta[p] for p in prob]
        gram = [_mm(jnp.concatenate([kb[p], qc[p]], axis=0), kc[p], NT, GDN_MM_GRAM) for p in prob]
        a = [-jnp.where(strict, gram[p][:chunk] * decay[p], 0.0) for p in prob]
        sol = [jnp.concatenate([vs[rows[p], hsl[p]] * beta[p], kb[p] * eg[p]], axis=1) for p in prob]
        for lvl in range(n_levels):
            mode = GDN_MM_SOLVE if lvl < GDN_SOLVE_FINE_LEVELS else "bf16"
            sol = [sol[p] + _mm(a[p], sol[p], NN, mode) for p in prob]
            if lvl + 1 < n_levels:
                a = [_mm(a[p], a[p], NN, "bf16") for p in prob]
        for p in prob:
            qks[hd[p], rows[p], :] = jnp.where(causal, gram[p][chunk:] * decay[p], 0.0)
            us[rows[p], hsl[p]] = sol[p][:, :GDN_DV]
            ws[rows[p], hsl[p]] = sol[p][:, GDN_DV:]
            qes[rows[p], hsl[p]] = qc[p] * eg[p]
            kds[rows[p], hsl[p]] = kc[p] * jnp.exp(gc[p][chunk - 1:chunk, :] - gc[p])
        return carry

    lax.fori_loop(0, ncb // cpi, factor_step, 0)

    def state_step(c, carry):
        r0 = pl.multiple_of(c * chunk, chunk)
        rows = pl.ds(r0, chunk)
        g_last_all = gcc[pl.ds(r0 + chunk - 1, 1), :]
        heads = range(GDN_HEADS)
        hsl = [slice(h * GDN_DK, (h + 1) * GDN_DK) for h in heads]
        s_prev = [s_scr[h][...] for h in heads]
        v_new = [us[rows, hsl[h]] - _mm(ws[rows, hsl[h]], s_prev[h], NN, GDN_MM_STATE) for h in heads]
        s_add = [_mm(kds[rows, hsl[h]], v_new[h], TN, GDN_MM_STATE) for h in heads]
        for h in heads:
            s_scr[h][...] = s_prev[h] * jnp.exp(g_last_all[:, 12 + h:13 + h]) + s_add[h]
        o = [_mm(qes[rows, hsl[h]], s_prev[h], NN, GDN_MM_STATE) + _mm(qks[h, rows, :], v_new[h], NN, GDN_MM_STATE)
             for h in heads]
        for h in heads:
            og = o[h] * lax.rsqrt(jnp.mean(o[h] * o[h], -1, keepdims=True) + NORM_EPS) * nw_ref[...]
            if ti < tt:
                o_ref[:, hsl[h]] = og[:ti] * _silu(z_ref[:, hsl[h]])
            else:
                o_ref[rows, hsl[h]] = og * _silu(z_ref[rows, hsl[h]])
        return carry

    lax.fori_loop(0, ncb, state_step, 0)

    @pl.when(i == n_i - 1)
    def _():
        for h in range(GDN_HEADS):
            sT_ref[0, h] = s_scr[h][...]


def _gdn(qkv, z, small, gates_t, conv_w, a_log, dt_bias, norm_w, conv0, s0, B, T):
    chunk = GDN_CHUNK
    ti = min(ROW_TILE, T)
    tt = max(ti, chunk)
    n_i = T // ti
    ncb = tt // chunk
    pad = (0, LANES - 16)
    alr = jnp.pad(jnp.concatenate([jnp.zeros((12,), F32), a_log]), pad).reshape(1, LANES)
    dtr = jnp.pad(jnp.concatenate([jnp.zeros((12,), F32), dt_bias]), pad).reshape(1, LANES)
    alc = jnp.concatenate([jnp.zeros((4,), F32), a_log]).reshape(8, 1)
    dtc = jnp.concatenate([jnp.zeros((4,), F32), dt_bias]).reshape(8, 1)
    nw = norm_w.reshape(1, GDN_DV)
    C = GDN_CONV_DIM
    W = GDN_WIDTH

    def const(shape):
        return pl.BlockSpec(shape, lambda b, i: (0,) * len(shape))

    o, s_fin, c_fin = pl.pallas_call(
        functools.partial(_gdn_body, ti=ti, tt=tt, chunk=chunk),
        grid=(B, n_i),
        in_specs=[
            pl.BlockSpec((ti, C), lambda b, i: (b * n_i + i, 0)),
            pl.BlockSpec((ti, W), lambda b, i: (b * n_i + i, 0)),
            pl.BlockSpec((ti, LANES), lambda b, i: (b * n_i + i, 0)),
            pl.BlockSpec((ncb, 8, chunk), lambda b, i: (b * n_i + i, 0, 0)),
            const((CONV_K, C)), const((1, LANES)), const((1, LANES)), const((8, 1)), const((8, 1)),
            const((1, GDN_DV)),
            pl.BlockSpec((1, CONV_K - 1, C), lambda b, i: (b, 0, 0)),
            pl.BlockSpec((1, GDN_HEADS, GDN_DK, GDN_DV), lambda b, i: (b, 0, 0, 0)),
        ],
        out_specs=[
            pl.BlockSpec((ti, W), lambda b, i: (b * n_i + i, 0)),
            pl.BlockSpec((1, GDN_HEADS, GDN_DK, GDN_DV), lambda b, i: (b, 0, 0, 0)),
            pl.BlockSpec((1, CONV_K - 1, C), lambda b, i: (b, 0, 0)),
        ],
        out_shape=[
            jax.ShapeDtypeStruct((B * T, W), F32),
            jax.ShapeDtypeStruct((B, GDN_HEADS, GDN_DK, GDN_DV), F32),
            jax.ShapeDtypeStruct((B, CONV_K - 1, C), F32),
        ],
        scratch_shapes=[
            pltpu.VMEM((ti + 8, C), F32),
            pltpu.VMEM((tt, W), F32), pltpu.VMEM((tt, W), F32), pltpu.VMEM((tt, W), F32),
            pltpu.VMEM((tt, LANES), F32), pltpu.VMEM((tt, LANES), F32),
            pltpu.VMEM((tt, LANES), F32), pltpu.VMEM((ncb, 8, chunk), F32),
            pltpu.VMEM((tt, W), F32), pltpu.VMEM((tt, W), F32),
            pltpu.VMEM((tt, W), F32), pltpu.VMEM((tt, W), F32),
            pltpu.VMEM((GDN_HEADS, tt, chunk), F32),
        ] + [pltpu.VMEM((GDN_DK, GDN_DV), F32)] * GDN_HEADS,
        compiler_params=_cparams(("parallel", "arbitrary")),
        name="gated_delta",
    )(qkv, z, small, gates_t, conv_w, alr, dtr, alc, dtc, nw, conv0, s0)
    return o, s_fin, c_fin


def _gates_transposed(small, B, T):
    chunk = GDN_CHUNK
    g = small[:, IDX_HEADS:IDX_HEADS + 2 * GDN_HEADS].reshape(B, T, 2 * GDN_HEADS)
    tpad = -(-T // chunk) * chunk
    g = jnp.pad(g, ((0, 0), (0, tpad - T), (0, 0)))
    return g.reshape(B * tpad // chunk, chunk, 2 * GDN_HEADS).transpose(0, 2, 1)


def _rope_tables(pos):
    d_rot = HEAD_DIM // 4
    half = d_rot // 2
    inv_freq = ROPE_THETA ** (-jnp.arange(half, dtype=F32) / half)
    ang = pos.astype(F32)[:, None] * inv_freq[None, :]
    cos = jnp.cos(ang)
    sin = jnp.sin(ang)
    n = pos.shape[0]
    ones = jnp.ones((n, HEAD_DIM - d_rot), F32)
    zeros = jnp.zeros((n, HEAD_DIM - d_rot), F32)
    zh = jnp.zeros((n, half), F32)
    c = jnp.concatenate([cos, cos, ones], axis=1)
    sa = jnp.concatenate([-sin, zh, zeros], axis=1)
    sb = jnp.concatenate([zh, sin, zeros], axis=1)
    rep = LANES // HEAD_DIM
    return jnp.tile(c, (1, rep)), jnp.tile(sa, (1, rep)), jnp.tile(sb, (1, rep))


def _split_w_in(w_in):
    pts = [0]
    for s in SPLITS:
        pts.append(pts[-1] + s)
    wq, wk, wv, wqi, wki, wwi, wqkv, wz, wb, wa = [w_in[:, pts[n]:pts[n + 1]] for n in range(len(SPLITS))]
    D = w_in.shape[0]
    wr = jnp.concatenate([wq, wqi, wk, wki, jnp.zeros((D, ROPE_W - 2 * ATT_WIDTH - LANES - IDX_DIM), F32)], axis=1)
    ws = jnp.concatenate([wwi, wb, wa, jnp.zeros((D, LANES - IDX_HEADS - 2 * GDN_HEADS), F32)], axis=1)
    return tuple(a.astype(BF16) for a in (wr, wv, ws, wqkv, wz))


def kernel(x_prompt, x_sample, cache_k, cache_v, cache_kidx, state_gdn, state_conv, page_table, ffn1_w_gate, ffn1_w_up, ffn1_w_down, ln1_g, ln1_b, w_in, conv_w, a_log, dt_bias, gdn_norm_w, w_o, ln2_g, ln2_b, ffn2_w_gate, ffn2_w_up, ffn2_w_down, ln3_g, ln3_b):
    B, S, D = x_prompt.shape
    Bs, T, _ = x_sample.shape
    n_pages = page_table.shape[1]
    page = cache_k.shape[2]
    past = n_pages * page
    kvw = N_KV_HEADS * HEAD_DIM
    l = 0
    xp = x_prompt.reshape(B * S, D)
    xs = x_sample.reshape(Bs * T, D)

    f1 = (ffn1_w_gate[l].astype(BF16), ffn1_w_up[l].astype(BF16), ffn1_w_down[l].astype(BF16), ln1_g, ln1_b)
    f2 = (ffn2_w_gate[l].astype(BF16), ffn2_w_up[l].astype(BF16), ffn2_w_down[l].astype(BF16), ln3_g, ln3_b)
    w_proj = _split_w_in(w_in[l])
    wo_att = w_o[l][:ATT_WIDTH].astype(BF16)
    wo_gdn = w_o[l][ATT_WIDTH:].astype(BF16)

    xp1 = _ffn_half_step(xp, *f1)
    tm_p = min(ROW_TILE, B * S)
    q, qi, kt, kit, vt, small, qkv, z, kt_bf, kit_bf, vt_bf = _mix_projection(
        xp1, w_proj, _rope_tables(jnp.arange(S)), S // tm_p, seq=S)
    att = _dsa_prompt(q, qi, small, kt_bf, kit_bf, vt_bf)
    gdn, s_p, c_p = _gdn(qkv, z, small, _gates_transposed(small, B, S), conv_w[l], a_log[l], dt_bias[l],
                         gdn_norm_w[l], jnp.zeros((B, CONV_K - 1, GDN_CONV_DIM), F32),
                         jnp.zeros((B, GDN_HEADS, GDN_DK, GDN_DV), F32), B, S)
    y_prompt = _ffn_half_step(xp1, *f2, mix=(att, gdn, wo_att, wo_gdn, ln2_g, ln2_b)).reshape(B, S, D)
    k_prompt = kt.reshape(1, B, N_KV_HEADS, HEAD_DIM, S).transpose(0, 1, 4, 2, 3)
    v_prompt = vt.reshape(1, B, N_KV_HEADS, HEAD_DIM, S).transpose(0, 1, 4, 2, 3)
    kidx_prompt = kit.reshape(1, B, IDX_DIM, S).transpose(0, 1, 3, 2)

    xs1 = _ffn_half_step(xs, *f1)
    pos_s = jnp.tile(past + jnp.arange(T), Bs)
    tm_s = min(ROW_TILE, Bs * T)
    q, qi, k, ki, v, small, qkv, z = _mix_projection(xs1, w_proj, _rope_tables(pos_s), Bs * T // tm_s)
    group = ATT_HEADS // N_KV_HEADS
    qh = q.reshape(Bs, T, N_KV_HEADS, group, HEAD_DIM).transpose(0, 2, 3, 1, 4)
    zq = jnp.zeros_like(qh[:, 0])
    q_ht = jnp.concatenate([jnp.concatenate([qh[:, 0], zq], -1), jnp.concatenate([zq, qh[:, 1]], -1)], axis=1)
    q_ht = q_ht.reshape(Bs, ATT_HEADS * T, LANES)
    qi_ht = qi.reshape(Bs, T, IDX_HEADS, IDX_DIM).transpose(0, 2, 1, 3).reshape(Bs, IDX_HEADS * T, IDX_DIM)

    def new_tile(a):
        a = a.reshape(Bs, T, a.shape[-1]).transpose(0, 2, 1)
        return jnp.pad(a, ((0, 0), (0, 0), (0, LANES - T)))

    cki_t = cache_kidx[l].transpose(0, 2, 1)
    ck_t = cache_k[l].transpose(0, 2, 3, 1).reshape(-1, kvw, page)
    cv_t = cache_v[l].transpose(0, 2, 3, 1).reshape(-1, kvw, page)
    o_ht = _dsa_sample(page_table, qi_ht, small.reshape(Bs, T, LANES), q_ht, new_tile(ki), new_tile(k), new_tile(v),
                       cki_t, ck_t, cv_t)
    o_ht = o_ht.reshape(Bs, N_KV_HEADS, group, T, N_KV_HEADS, HEAD_DIM)
    att = jnp.stack([o_ht[:, 0, :, :, 0], o_ht[:, 1, :, :, 1]], axis=1)
    att = att.transpose(0, 3, 1, 2, 4).reshape(Bs * T, ATT_WIDTH)
    gdn, s_s, c_s = _gdn(qkv, z, small, _gates_transposed(small, Bs, T), conv_w[l], a_log[l], dt_bias[l],
                         gdn_norm_w[l], state_conv[l], state_gdn[l], Bs, T)
    y_sample = _ffn_half_step(xs1, *f2, mix=(att, gdn, wo_att, wo_gdn, ln2_g, ln2_b)).reshape(Bs, T, D)

    return (y_prompt, y_sample,
            k_prompt, v_prompt, kidx_prompt, s_p[None], c_p[None],
            k.reshape(1, Bs, T, N_KV_HEADS, HEAD_DIM), v.reshape(1, Bs, T, N_KV_HEADS, HEAD_DIM),
            ki.reshape(1, Bs, T, IDX_DIM), s_s[None], c_s[None])
```
